```python
import jax, jax.numpy as jnp
from jax import lax
import numpy as np

D_MODEL = 1024
BATCH = 4
SEQ = 4096
DEPTH = 1
DEC_BATCH = 128
DEC_SEQ = 4
PAST_LEN = 8192
PAGE_SIZE = 128

SB_HEADS = 8
SB_HEAD_DIM = 64
SB_WIDTH = SB_HEADS * SB_HEAD_DIM
MLA_HEADS = 8
MLA_NOPE_DIM = 64
MLA_ROPE_DIM = 32
MLA_V_DIM = 64
MLA_QK_DIM = MLA_NOPE_DIM + MLA_ROPE_DIM
MLA_WIDTH = MLA_HEADS * MLA_V_DIM
Q_LORA = 256
KV_LORA = 128
MIX_WIDTH = SB_WIDTH + MLA_WIDTH
IN_COLS = 3 * SB_WIDTH + Q_LORA + KV_LORA + MLA_ROPE_DIM
SPLIT_POINTS = (SB_WIDTH, 2 * SB_WIDTH, 3 * SB_WIDTH, 3 * SB_WIDTH + Q_LORA, 3 * SB_WIDTH + Q_LORA + KV_LORA)
D_FF = 2816
FFN_RESIDUAL_WEIGHT = 0.5
Q_BLOCK = 128
ROPE_THETA = 10000.0
NORM_EPS = 1e-6
SB_SCALE = SB_HEAD_DIM ** -0.5
MLA_SCALE = MLA_QK_DIM ** -0.5
NEG_INF = -1e30

kernel_name = 'hymba_stickbreak_mla_macaron_step'


def rms_norm(x, g):
    xf = x.astype(jnp.float32)
    y = xf * lax.rsqrt(jnp.mean(xf * xf, axis=-1, keepdims=True) + NORM_EPS)
    return (y * g.astype(jnp.float32)).astype(x.dtype)


def rope(x, pos):
    half = MLA_ROPE_DIM // 2
    inv_freq = ROPE_THETA ** (-jnp.arange(half, dtype=jnp.float32) / half)
    ang = pos.astype(jnp.float32)[:, None] * inv_freq[None, :]
    cos = jnp.cos(ang)[None, :, None, :].astype(x.dtype)
    sin = jnp.sin(ang)[None, :, None, :].astype(x.dtype)
    x1, x2 = x[..., :half], x[..., half:]
    return jnp.concatenate([x1 * cos - x2 * sin, x2 * cos + x1 * sin], axis=-1)


def half_step_ffn(x, g_pre, w_gate, w_up, w_down, g_post):
    h = rms_norm(x, g_pre)
    y = (jax.nn.silu(h @ w_gate) * (h @ w_up)) @ w_down
    return x + FFN_RESIDUAL_WEIGHT * rms_norm(y, g_post)


def stick_breaking_weights(z, mask):
    z = z.astype(jnp.float32) * SB_SCALE
    log_keep = jnp.where(mask, jax.nn.log_sigmoid(-z), 0.0)
    log_after = lax.cumsum(log_keep, axis=z.ndim - 1, reverse=True) - log_keep
    return jnp.where(mask, jnp.exp(jax.nn.log_sigmoid(z) + log_after), 0.0)


def causal_softmax(s, mask):
    s = jnp.where(mask, s.astype(jnp.float32) * MLA_SCALE, NEG_INF)
    return jax.nn.softmax(s, axis=-1)


def mixer_project(h, pos, w_in, g_q_lora, w_q_up, g_kv_lora, w_uk):
    b, t, _ = h.shape
    sb_q, sb_k, sb_v, q_down, kv_down, kpe_raw = jnp.split(h @ w_in, SPLIT_POINTS, axis=-1)
    heads = lambda a: a.reshape(b, t, SB_HEADS, SB_HEAD_DIM)
    q = (rms_norm(q_down, g_q_lora) @ w_q_up).reshape(b, t, MLA_HEADS, MLA_QK_DIM)
    q_nope, q_pe = q[..., :MLA_NOPE_DIM], q[..., MLA_NOPE_DIM:]
    q_lat = jnp.einsum('bthn,chn->bthc', q_nope, w_uk)
    q_pe = rope(q_pe, pos)
    ckv = rms_norm(kv_down, g_kv_lora)
    kpe = rope(kpe_raw[:, :, None, :], pos)[:, :, 0, :]
    return heads(sb_q), heads(sb_k), heads(sb_v), q_lat, q_pe, ckv, kpe


def mixer_output(o_sb, o_lat, w_uv, g_sb_out, g_mla_out, w_out):
    b, t = o_sb.shape[:2]
    o_mla = jnp.einsum('bthc,chv->bthv', o_lat, w_uv).reshape(b, t, MLA_WIDTH)
    merged = jnp.concatenate([rms_norm(o_sb.reshape(b, t, SB_WIDTH), g_sb_out), rms_norm(o_mla, g_mla_out)], axis=-1)
    return merged @ w_out


def prompt_mixer(h, w_in, g_q_lora, w_q_up, g_kv_lora, w_uk, w_uv, g_sb_out, g_mla_out, w_out):
    b, t, _ = h.shape
    nb = t // Q_BLOCK
    pos = jnp.arange(t)
    sb_q, sb_k, sb_v, q_lat, q_pe, ckv, kpe = mixer_project(h, pos, w_in, g_q_lora, w_q_up, g_kv_lora, w_uk)

    def attend_block(blk):
        bq_sb, bq_lat, bq_pe, bpos = blk
        strict = pos[None, :] < bpos[:, None]
        a = stick_breaking_weights(jnp.einsum('bqhd,bkhd->bhqk', bq_sb, sb_k), strict).astype(sb_v.dtype)
        o_sb = jnp.einsum('bhqk,bkhd->bqhd', a, sb_v)
        s = jnp.einsum('bqhc,bkc->bhqk', bq_lat, ckv) + jnp.einsum('bqhr,bkr->bhqk', bq_pe, kpe)
        p = causal_softmax(s, pos[None, :] <= bpos[:, None]).astype(ckv.dtype)
        o_lat = jnp.einsum('bhqk,bkc->bqhc', p, ckv)
        return o_sb, o_lat

    to_blocks = lambda a: jnp.swapaxes(a.reshape(b, nb, Q_BLOCK, *a.shape[2:]), 0, 1)
    from_blocks = lambda a: jnp.swapaxes(a, 0, 1).reshape(b, t, *a.shape[3:])
    o_sb, o_lat = lax.map(attend_block, (to_blocks(sb_q), to_blocks(q_lat), to_blocks(q_pe), pos.reshape(nb, Q_BLOCK)))
    y = mixer_output(from_blocks(o_sb), from_blocks(o_lat), w_uv, g_sb_out, g_mla_out, w_out)
    return y, (sb_k, sb_v, ckv, kpe)


def sample_mixer(h, cache_sb_k, cache_sb_v, cache_mla_ckv, cache_mla_kpe, page_table,
                 w_in, g_q_lora, w_q_up, g_kv_lora, w_uk, w_uv, g_sb_out, g_mla_out, w_out):
    b, t, _ = h.shape
    pos = PAST_LEN + jnp.arange(t)
    sb_q, sb_k, sb_v, q_lat, q_pe, ckv, kpe = mixer_project(h, pos, w_in, g_q_lora, w_q_up, g_kv_lora, w_uk)
    gather = lambda pool: pool[page_table].reshape(b, PAST_LEN, *pool.shape[2:])
    past_k, past_v = gather(cache_sb_k), gather(cache_sb_v)
    past_ckv, past_kpe = gather(cache_mla_ckv), gather(cache_mla_kpe)
    k_pos = jnp.arange(PAST_LEN + t)
    strict = k_pos[None, :] < pos[:, None]
    inclusive = k_pos[None, :] <= pos[:, None]
    z = jnp.concatenate([jnp.einsum('bqhd,bkhd->bhqk', sb_q, past_k), jnp.einsum('bqhd,bkhd->bhqk', sb_q, sb_k)], axis=-1)
    a = stick_breaking_weights(z, strict).astype(sb_v.dtype)
    o_sb = jnp.einsum('bhqk,bkhd->bqhd', a[..., :PAST_LEN], past_v) + jnp.einsum('bhqk,bkhd->bqhd', a[..., PAST_LEN:], sb_v)
    s = jnp.concatenate([
        jnp.einsum('bqhc,bkc->bhqk', q_lat, past_ckv) + jnp.einsum('bqhr,bkr->bhqk', q_pe, past_kpe),
        jnp.einsum('bqhc,bkc->bhqk', q_lat, ckv) + jnp.einsum('bqhr,bkr->bhqk', q_pe, kpe)], axis=-1)
    p = causal_softmax(s, inclusive).astype(ckv.dtype)
    o_lat = jnp.einsum('bhqk,bkc->bqhc', p[..., :PAST_LEN], past_ckv) + jnp.einsum('bhqk,bkc->bqhc', p[..., PAST_LEN:], ckv)
    y = mixer_output(o_sb, o_lat, w_uv, g_sb_out, g_mla_out, w_out)
    return y, (sb_k, sb_v, ckv, kpe)


def setup_inputs(seed: int = 0) -> dict:
    key = jax.random.key(seed)
    ks = iter(jax.random.split(key, 32))
    nrm = lambda shape, scale: jax.random.normal(next(ks), shape, jnp.float32) * scale
    gain = lambda n: 1.0 + 0.02 * jax.random.normal(next(ks), (n,), jnp.float32)
    n_pages = PAST_LEN // PAGE_SIZE
    n_used = DEC_BATCH * n_pages
    n_pool = n_used + (n_used + 3) // 4
    inp = {}
    inp['x_prompt'] = nrm((BATCH, SEQ, D_MODEL), 1.0)
    inp['x_sample'] = nrm((DEC_BATCH, DEC_SEQ, D_MODEL), 1.0)
    inp['cache_sb_k'] = nrm((n_pool, PAGE_SIZE, SB_HEADS, SB_HEAD_DIM), 1.0)
    inp['cache_sb_v'] = nrm((n_pool, PAGE_SIZE, SB_HEADS, SB_HEAD_DIM), 1.0)
    inp['cache_mla_ckv'] = nrm((n_pool, PAGE_SIZE, KV_LORA), 1.0)
    inp['cache_mla_kpe'] = nrm((n_pool, PAGE_SIZE, MLA_ROPE_DIM), 1.0)
    inp['page_table'] = jax.random.permutation(next(ks), n_pool)[:n_used].reshape(DEC_BATCH, n_pages).astype(jnp.int32)
    inp['g_ffn1_pre'] = gain(D_MODEL)
    inp['w_ffn1_gate'] = nrm((D_MODEL, D_FF), D_MODEL ** -0.5)
    inp['w_ffn1_up'] = nrm((D_MODEL, D_FF), D_MODEL ** -0.5)
    inp['w_ffn1_down'] = nrm((D_FF, D_MODEL), D_FF ** -0.5)
    inp['g_ffn1_post'] = gain(D_MODEL)
    inp['g_mix_pre'] = gain(D_MODEL)
    inp['w_in'] = nrm((D_MODEL, IN_COLS), D_MODEL ** -0.5)
    inp['g_q_lora'] = gain(Q_LORA)
    inp['w_q_up'] = nrm((Q_LORA, MLA_HEADS * MLA_QK_DIM), Q_LORA ** -0.5)
    inp['g_kv_lora'] = gain(KV_LORA)
    inp['w_uk'] = nrm((KV_LORA, MLA_HEADS, MLA_NOPE_DIM), KV_LORA ** -0.5)
    inp['w_uv'] = nrm((KV_LORA, MLA_HEADS, MLA_V_DIM), KV_LORA ** -0.5)
    inp['g_sb_out'] = gain(SB_WIDTH)
    inp['g_mla_out'] = gain(MLA_WIDTH)
    inp['w_out'] = nrm((MIX_WIDTH, D_MODEL), MIX_WIDTH ** -0.5)
    inp['g_mix_post'] = gain(D_MODEL)
    inp['g_ffn2_pre'] = gain(D_MODEL)
    inp['w_ffn2_gate'] = nrm((D_MODEL, D_FF), D_MODEL ** -0.5)
    inp['w_ffn2_up'] = nrm((D_MODEL, D_FF), D_MODEL ** -0.5)
    inp['w_ffn2_down'] = nrm((D_FF, D_MODEL), D_FF ** -0.5)
    inp['g_ffn2_post'] = gain(D_MODEL)
    return inp


def reference(x_prompt, x_sample, cache_sb_k, cache_sb_v, cache_mla_ckv, cache_mla_kpe, page_table,
              g_ffn1_pre, w_ffn1_gate, w_ffn1_up, w_ffn1_down, g_ffn1_post,
              g_mix_pre, w_in, g_q_lora, w_q_up, g_kv_lora, w_uk, w_uv, g_sb_out, g_mla_out, w_out, g_mix_post,
              g_ffn2_pre, w_ffn2_gate, w_ffn2_up, w_ffn2_down, g_ffn2_post):
    ffn1 = (g_ffn1_pre, w_ffn1_gate, w_ffn1_up, w_ffn1_down, g_ffn1_post)
    ffn2 = (g_ffn2_pre, w_ffn2_gate, w_ffn2_up, w_ffn2_down, g_ffn2_post)
    mix_w = (w_in, g_q_lora, w_q_up, g_kv_lora, w_uk, w_uv, g_sb_out, g_mla_out, w_out)
    xp, xs = x_prompt, x_sample
    for _ in range(DEPTH):
        xp = half_step_ffn(xp, *ffn1)
        o_p, (sb_k_p, sb_v_p, ckv_p, kpe_p) = prompt_mixer(rms_norm(xp, g_mix_pre), *mix_w)
        xp = xp + rms_norm(o_p, g_mix_post)
        xp = half_step_ffn(xp, *ffn2)
        xs = half_step_ffn(xs, *ffn1)
        o_s, (sb_k_s, sb_v_s, ckv_s, kpe_s) = sample_mixer(rms_norm(xs, g_mix_pre), cache_sb_k, cache_sb_v,
                                                           cache_mla_ckv, cache_mla_kpe, page_table, *mix_w)
        xs = xs + rms_norm(o_s, g_mix_post)
        xs = half_step_ffn(xs, *ffn2)
    return (xp, xs, sb_k_p, sb_v_p, ckv_p, kpe_p, sb_k_s, sb_v_s, ckv_s, kpe_s)
```

```python
import functools

import jax
import jax.numpy as jnp
from jax import lax
from jax.experimental import pallas as pl
from jax.experimental.pallas import tpu as pltpu

F32 = jnp.float32
BF16 = jnp.bfloat16

SB_HEADS = 8
SB_HEAD_DIM = 64
SB_WIDTH = SB_HEADS * SB_HEAD_DIM
SB_PAIRS = SB_HEADS // 2
MLA_HEADS = 8
MLA_NOPE_DIM = 64
MLA_ROPE_DIM = 32
MLA_V_DIM = 64
MLA_QK_DIM = MLA_NOPE_DIM + MLA_ROPE_DIM
MLA_WIDTH = MLA_HEADS * MLA_V_DIM
Q_LORA = 256
KV_LORA = 128
FFN_RESIDUAL_WEIGHT = 0.5
ROPE_THETA = 10000.0
NORM_EPS = 1e-6
SB_SCALE = SB_HEAD_DIM ** -0.5
MLA_SCALE = MLA_QK_DIM ** -0.5
NEG_INF = -1e30

LANES = 128
SUBLANES = 8
MXU_DIM = 256
VMEM_LIMIT_BYTES = 56 * 1024 * 1024

FFN_CHUNK = MXU_DIM
ATTN_BLOCK = MXU_DIM
QCAT = 2 * LANES
SB_EXIT = 104.0

SBD_GROUP = 4
SBD_PAGES = 2
MLAD_PAGES = 16


def _dot(a, b):
    return jnp.dot(a, b, preferred_element_type=F32)


def _dot_nt(a, b):
    return lax.dot_general(a, b, (((1,), (1,)), ((), ())), preferred_element_type=F32)


def _rms(x, g):
    ms = jnp.mean(x * x, axis=-1, keepdims=True)
    return x * lax.rsqrt(ms + NORM_EPS) * g


def _softplus(z):
    return jnp.maximum(z, 0.0) + jnp.log(1.0 + jnp.exp(-jnp.abs(z)))


def _split_bf16(x):
    hi = x.astype(BF16)
    lo = (x - hi.astype(F32)).astype(BF16)
    return hi, lo


def _suffix_sums(x, tri):
    hi, lo = _split_bf16(x)
    return _dot(hi, tri) + _dot(lo, tri)


def _tri(n):
    row = lax.broadcasted_iota(jnp.int32, (n, n), 0)
    col = lax.broadcasted_iota(jnp.int32, (n, n), 1)
    return (row >= col).astype(BF16)


def _const_spec(shape):
    zeros = (0,) * len(shape)
    return pl.BlockSpec(shape, lambda *_: zeros, pipeline_mode=pl.Buffered(1))


def _half_step_ffn(x, g_pre, wgu_ref, wd_ref, g_post):
    h = _rms(x, g_pre).astype(BF16)
    n_chunks = wgu_ref.shape[0]

    def body(c, acc):
        gu = _dot(h, wgu_ref[c])
        g = gu[:, :FFN_CHUNK]
        u = gu[:, FFN_CHUNK:]
        a = (g * jax.nn.sigmoid(g)) * u
        return acc + _dot(a.astype(BF16), wd_ref[c])

    y = lax.fori_loop(0, n_chunks, body, jnp.zeros(x.shape, F32))
    return x + FFN_RESIDUAL_WEIGHT * _rms(y, g_post)


_IN_Q = 0
_IN_K = SB_WIDTH
_IN_V = 2 * SB_WIDTH
_IN_QD = 3 * SB_WIDTH
_IN_KVD = _IN_QD + Q_LORA
_IN_KPE = _IN_KVD + KV_LORA
_IN_KPE_SWAP = _IN_KPE + LANES
_IN_COLS_EXT = _IN_KPE_SWAP + LANES
_QUP_NOPE = 0
_QUP_PE = MLA_HEADS * MLA_NOPE_DIM
_QUP_PE_SWAP = _QUP_PE + MLA_HEADS * LANES
_QUP_COLS_EXT = _QUP_PE_SWAP + MLA_HEADS * LANES


def _ffn_proj_kernel(x_ref, cos_ref, sin_ref, g_pre_ref, wgu_ref, wd_ref, g_post_ref,
                     g_mix_ref, w_in_ref, g_q_ref, w_qup_ref, w_uk_ref, g_kv_ref,
                     x1_ref, sbk_ref, sbv_ref, ckv_ref, kpe_ref,
                     qa_ref, ka_ref, va_ref, qcat_ref, kcat_ref):
    x1 = _half_step_ffn(x_ref[0], g_pre_ref[...], wgu_ref, wd_ref, g_post_ref[...])
    x1_ref[0] = x1

    hm = _rms(x1, g_mix_ref[...]).astype(BF16)
    proj = _dot(hm, w_in_ref[...])
    sbk = proj[:, _IN_K:_IN_K + SB_WIDTH]
    sbv = proj[:, _IN_V:_IN_V + SB_WIDTH]
    sbk_ref[0] = sbk
    sbv_ref[0] = sbv
    for p in range(SB_PAIRS):
        lo, hi = p * LANES, (p + 1) * LANES
        qa_ref[0, p] = proj[:, _IN_Q + lo:_IN_Q + hi].astype(BF16)
        ka_ref[0, p] = sbk[:, lo:hi].astype(BF16)
        va_ref[0, p] = sbv[:, lo:hi].astype(BF16)

    cos = cos_ref[...]
    sin = sin_ref[...]
    qn = _rms(proj[:, _IN_QD:_IN_QD + Q_LORA], g_q_ref[...]).astype(BF16)
    qup = _dot(qn, w_qup_ref[...])
    qlat = _dot(qup[:, _QUP_NOPE:_QUP_PE].astype(BF16), w_uk_ref[...])
    for h in range(MLA_HEADS):
        lo, hi = h * LANES, (h + 1) * LANES
        pe = (qup[:, _QUP_PE + lo:_QUP_PE + hi] * cos
              + qup[:, _QUP_PE_SWAP + lo:_QUP_PE_SWAP + hi] * sin)
        qcat_ref[0, h, :, :LANES] = qlat[:, lo:hi].astype(BF16)
        qcat_ref[0, h, :, LANES:] = pe.astype(BF16)

    ckv = _rms(proj[:, _IN_KVD:_IN_KVD + KV_LORA], g_kv_ref[...])
    kpe = (proj[:, _IN_KPE:_IN_KPE + LANES] * cos
           + proj[:, _IN_KPE_SWAP:_IN_KPE_SWAP + LANES] * sin)
    ckv_ref[0] = ckv
    kpe_ref[0] = kpe[:, :MLA_ROPE_DIM]
    kcat_ref[0, :, :LANES] = ckv.astype(BF16)
    kcat_ref[0, :, LANES:] = kpe.astype(BF16)


def _ffn_proj_call(x, cos_t, sin_t, ffn_w, proj_w, tm):
    nb, t, d = x.shape
    g_pre, wgu, wd, g_post = ffn_w
    g_mix, w_in, g_q, w_qup, w_uk, g_kv = proj_w
    tok = lambda shape: pl.BlockSpec((1, tm) + shape, lambda b, i: (b, i) + (0,) * len(shape))
    heads = lambda n, w: pl.BlockSpec((1, n, tm, w), lambda b, i: (b, 0, i, 0))
    tab = pl.BlockSpec((tm, LANES), lambda b, i: (i, 0))
    in_specs = [tok((d,)), tab, tab] + [_const_spec(w.shape) for w in
                                          (g_pre, wgu, wd, g_post, g_mix, w_in, g_q, w_qup, w_uk, g_kv)]
    out_shape = (
        jax.ShapeDtypeStruct((nb, t, d), F32),
        jax.ShapeDtypeStruct((nb, t, SB_WIDTH), F32),
        jax.ShapeDtypeStruct((nb, t, SB_WIDTH), F32),
        jax.ShapeDtypeStruct((nb, t, KV_LORA), F32),
        jax.ShapeDtypeStruct((nb, t, MLA_ROPE_DIM), F32),
        jax.ShapeDtypeStruct((nb, SB_PAIRS, t, LANES), BF16),
        jax.ShapeDtypeStruct((nb, SB_PAIRS, t, LANES), BF16),
        jax.ShapeDtypeStruct((nb, SB_PAIRS, t, LANES), BF16),
        jax.ShapeDtypeStruct((nb, MLA_HEADS, t, QCAT), BF16),
        jax.ShapeDtypeStruct((nb, t, QCAT), BF16),
    )
    out_specs = (tok((d,)), tok((SB_WIDTH,)), tok((SB_WIDTH,)), tok((KV_LORA,)), tok((MLA_ROPE_DIM,)),
                 heads(SB_PAIRS, LANES), heads(SB_PAIRS, LANES), heads(SB_PAIRS, LANES),
                 heads(MLA_HEADS, QCAT), tok((QCAT,)))
    return pl.pallas_call(
        _ffn_proj_kernel,
        grid=(nb, t // tm),
        in_specs=in_specs,
        out_specs=out_specs,
        out_shape=out_shape,
        compiler_params=pltpu.CompilerParams(
            dimension_semantics=("parallel", "parallel"), vmem_limit_bytes=VMEM_LIMIT_BYTES),
        name="ffn_proj",
    )(x, cos_t, sin_t, g_pre, wgu, wd, g_post, g_mix, w_in, g_q, w_qup, w_uk, g_kv)


def _out_ffn_kernel(osb_ref, olat_ref, x1_ref, g_sb_ref, g_mla_ref, w_uv_ref, w_out_ref, g_mixpost_ref,
                    g_pre_ref, wgu_ref, wd_ref, g_post_ref, y_ref):
    osb = jnp.concatenate([osb_ref[0, p] for p in range(SB_PAIRS)], axis=-1)
    olat = jnp.concatenate([olat_ref[0, h] for h in range(MLA_HEADS)], axis=-1)
    omla = _dot(olat.astype(BF16), w_uv_ref[...])
    merged = jnp.concatenate([_rms(osb, g_sb_ref[...]), _rms(omla, g_mla_ref[...])], axis=-1)
    o = _dot(merged.astype(BF16), w_out_ref[...])
    x2 = x1_ref[0] + _rms(o, g_mixpost_ref[...])
    y_ref[0] = _half_step_ffn(x2, g_pre_ref[...], wgu_ref, wd_ref, g_post_ref[...])


def _out_ffn_call(osb, olat, x1, out_w, ffn_w, tm):
    nb, t, d = x1.shape
    g_sb, g_mla, w_uv, w_out, g_mixpost = out_w
    g_pre, wgu, wd, g_post = ffn_w
    heads = lambda n: pl.BlockSpec((1, n, tm, LANES), lambda b, i: (b, 0, i, 0))
    tok = pl.BlockSpec((1, tm, d), lambda b, i: (b, i, 0))
    weights = (g_sb, g_mla, w_uv, w_out, g_mixpost, g_pre, wgu, wd, g_post)
    return pl.pallas_call(
        _out_ffn_kernel,
        grid=(nb, t // tm),
        in_specs=[heads(SB_PAIRS), heads(MLA_HEADS), tok] + [_const_spec(w.shape) for w in weights],
        out_specs=tok,
        out_shape=jax.ShapeDtypeStruct((nb, t, d), F32),
        compiler_params=pltpu.CompilerParams(
            dimension_semantics=("parallel", "parallel"), vmem_limit_bytes=VMEM_LIMIT_BYTES),
        name="out_ffn",
    )(osb, olat, x1, *weights)


def _sb_block(q, k, v, tri, carry, acc, valid):
    z = _dot_nt(q, k)
    sp = _softplus(z)
    if valid is not None:
        sp = jnp.where(valid, sp, 0.0)
    suf = _suffix_sums(sp, tri)
    a = jnp.exp(z - suf - carry)
    if valid is not None:
        a = jnp.where(valid, a, 0.0)
    acc = acc + _dot(a.astype(BF16), v)
    return carry + suf[:, 0:1], acc


def _prompt_attn_kernel(qa_ref, ka_ref, va_ref, qcat_ref, kcat_ref, osb_ref, olat_ref):
    qi = pl.program_id(1)
    blk = ATTN_BLOCK
    row = lax.broadcasted_iota(jnp.int32, (blk, blk), 0)
    col = lax.broadcasted_iota(jnp.int32, (blk, blk), 1)
    tri = (row >= col).astype(BF16)
    strict = col < row
    lane = lax.broadcasted_iota(jnp.int32, (blk, LANES), 1)
    even_lanes = lane < SB_HEAD_DIM
    diag = pl.multiple_of(qi * blk, blk)

    def pair_body(p, _):
        qp = qa_ref[0, p]
        zero = jnp.zeros_like(qp)
        q_even = jnp.where(even_lanes, qp, zero)
        q_odd = jnp.where(even_lanes, zero, qp)
        kd = ka_ref[0, p, pl.ds(diag, blk), :]
        vd = va_ref[0, p, pl.ds(diag, blk), :]
        c0 = jnp.zeros((blk, 1), F32)
        a0 = jnp.zeros((blk, LANES), F32)
        ce, ae = _sb_block(q_even, kd, vd, tri, c0, a0, strict)
        co, ao = _sb_block(q_odd, kd, vd, tri, c0, a0, strict)

        def cond(st):
            j, ce, _, co, _ = st
            live = jnp.minimum(jnp.min(ce), jnp.min(co)) < SB_EXIT
            return jnp.logical_and(j >= 0, live)

        def body(st):
            j, ce, ae, co, ao = st
            off = pl.multiple_of(j * blk, blk)
            kb = ka_ref[0, p, pl.ds(off, blk), :]
            vb = va_ref[0, p, pl.ds(off, blk), :]
            ce, ae = _sb_block(q_even, kb, vb, tri, ce, ae, None)
            co, ao = _sb_block(q_odd, kb, vb, tri, co, ao, None)
            return j - 1, ce, ae, co, ao

        _, _, ae, _, ao = lax.while_loop(cond, body, (qi - 1, ce, ae, co, ao))
        osb_ref[0, p] = jnp.where(even_lanes, ae, ao)
        return 0

    lax.fori_loop(0, SB_PAIRS, pair_body, 0)

    rows = MLA_HEADS * blk
    qc = qcat_ref[0].reshape(rows, QCAT)
    inclusive = (col <= row)[None]

    def mla_block(kc, m, l, acc, masked):
        s = _dot_nt(qc, kc) * MLA_SCALE
        if masked:
            s = jnp.where(inclusive, s.reshape(MLA_HEADS, blk, blk), NEG_INF).reshape(rows, blk)
        m_new = jnp.maximum(m, jnp.max(s, axis=-1, keepdims=True))
        p = jnp.exp(s - m_new)
        alpha = jnp.exp(m - m_new)
        l = alpha * l + jnp.sum(p, axis=-1, keepdims=True)
        acc = alpha * acc + _dot(p.astype(BF16), kc[:, :KV_LORA])
        return m_new, l, acc

    m0 = jnp.full((rows, 1), NEG_INF, F32)
    l0 = jnp.zeros((rows, 1), F32)
    acc0 = jnp.zeros((rows, KV_LORA), F32)
    st = mla_block(kcat_ref[0, pl.ds(diag, blk), :], m0, l0, acc0, True)

    def mla_body(j, st):
        off = pl.multiple_of(j * blk, blk)
        return mla_block(kcat_ref[0, pl.ds(off, blk), :], *st, False)

    _, l, acc = lax.fori_loop(0, qi, mla_body, st)
    olat_ref[0] = (acc / l).reshape(MLA_HEADS, blk, KV_LORA)


def _prompt_attn_call(qa, ka, va, qcat, kcat):
    nb, _, t, _ = qa.shape
    blk = ATTN_BLOCK
    q_spec = lambda n, w: pl.BlockSpec((1, n, blk, w), lambda b, i: (b, 0, i, 0))
    kv_spec = pl.BlockSpec((1, SB_PAIRS, t, LANES), lambda b, i: (b, 0, 0, 0))
    return pl.pallas_call(
        _prompt_attn_kernel,
        grid=(nb, t // blk),
        in_specs=[q_spec(SB_PAIRS, LANES), kv_spec, kv_spec, q_spec(MLA_HEADS, QCAT),
                  pl.BlockSpec((1, t, QCAT), lambda b, i: (b, 0, 0))],
        out_specs=(q_spec(SB_PAIRS, LANES), q_spec(MLA_HEADS, KV_LORA)),
        out_shape=(jax.ShapeDtypeStruct((nb, SB_PAIRS, t, LANES), F32),
                   jax.ShapeDtypeStruct((nb, MLA_HEADS, t, KV_LORA), F32)),
        compiler_params=pltpu.CompilerParams(
            dimension_semantics=("parallel", "parallel"), vmem_limit_bytes=VMEM_LIMIT_BYTES),
        name="prompt_attn",
    )(qa, ka, va, qcat, kcat)


def _sb_decode_kernel(pt_ref, q_ref, knew_hbm, vnew_hbm, kc_hbm, vc_hbm, o_ref,
                      kbuf, vbuf, kslow, vslow, sem, slow_sem, *, n_pages, n_new):
    step = pl.program_id(0)
    n_steps = pl.num_programs(0)
    grp = q_ref.shape[0]
    page = kc_hbm.shape[-1]
    first = SBD_PAGES * page
    ntok = first + page

    def first_copies(s, slot):
        out = []
        for g in range(grp):
            b = s * grp + g
            for hbm, new, buf, t in ((kc_hbm, knew_hbm, kbuf, 0), (vc_hbm, vnew_hbm, vbuf, 1)):
                for j in range(SBD_PAGES):
                    pid = pt_ref[b * n_pages + (n_pages - SBD_PAGES + j)]
                    out.append(pltpu.make_async_copy(
                        hbm.at[pid], buf.at[slot * grp + g, :, :, pl.ds(j * page, page)], sem.at[slot, t]))
                out.append(pltpu.make_async_copy(
                    new.at[b], buf.at[slot * grp + g, :, :, pl.ds(first, page)], sem.at[slot, t]))
        return out

    @pl.when(step == 0)
    def _():
        for c in first_copies(0, 0):
            c.start()

    slot = step % 2

    @pl.when(step + 1 < n_steps)
    def _():
        for c in first_copies(step + 1, 1 - slot):
            c.start()

    for c in first_copies(step, slot):
        c.wait()

    rows = SB_HEADS * SUBLANES
    r_iota = lax.broadcasted_iota(jnp.int32, (rows, ntok), 0)
    c_iota = lax.broadcasted_iota(jnp.int32, (rows, ntok), 1)
    valid = jnp.logical_or(c_iota < first, (c_iota - first) < (r_iota % SUBLANES))
    real_row = (lax.broadcasted_iota(jnp.int32, (rows, 1), 0) % SUBLANES) < n_new
    tri = _tri(ntok)
    tri_slow = _tri(first)

    def scores(qs, buf, idx):
        return jnp.concatenate([_dot(qs[h], buf[idx, h].astype(BF16)) for h in range(SB_HEADS)], axis=0)

    def weighted(a, buf, idx):
        return jnp.concatenate(
            [_dot_nt(a[h * SUBLANES:(h + 1) * SUBLANES], buf[idx, h].astype(BF16)) for h in range(SB_HEADS)],
            axis=0)

    for g in range(grp):
        b = step * grp + g
        idx = slot * grp + g
        qs = [q_ref[g, h].astype(BF16) for h in range(SB_HEADS)]
        z = scores(qs, kbuf, idx)
        sp = jnp.where(valid, _softplus(z), 0.0)
        suf = _suffix_sums(sp, tri)
        a = jnp.where(valid, jnp.exp(z - suf), 0.0).astype(BF16)
        carry = suf[:, 0:1]
        acc = weighted(a, vbuf, idx)

        def cond(st):
            c, carry, _ = st
            live = jnp.min(jnp.where(real_row, carry, SB_EXIT)) < SB_EXIT
            return jnp.logical_and(c >= 0, live)

        def body(st):
            c, carry, acc = st
            copies = []
            for hbm, buf, t in ((kc_hbm, kslow, 0), (vc_hbm, vslow, 1)):
                for j in range(SBD_PAGES):
                    pid = pt_ref[b * n_pages + c * SBD_PAGES + j]
                    copies.append(pltpu.make_async_copy(
                        hbm.at[pid], buf.at[0, :, :, pl.ds(j * page, page)], slow_sem.at[t]))
            for cp in copies:
                cp.start()
            for cp in copies:
                cp.wait()
            z = scores(qs, kslow, 0)
            suf = _suffix_sums(_softplus(z), tri_slow)
            a = jnp.exp(z - suf - carry).astype(BF16)
            return c - 1, carry + suf[:, 0:1], acc + weighted(a, vslow, 0)

        _, _, acc = lax.while_loop(cond, body, (n_pages // SBD_PAGES - 2, carry, acc))
        o_ref[g] = acc.reshape(SB_HEADS, SUBLANES, SB_HEAD_DIM)


def _sb_decode_call(pt_flat, q8, knew_t, vnew_t, cache_kt, cache_vt, n_pages, n_new):
    nbatch = q8.shape[0]
    page = cache_kt.shape[-1]
    grp = SBD_GROUP
    ntok = (SBD_PAGES + 1) * page
    any_spec = pl.BlockSpec(memory_space=pl.ANY)
    blk = pl.BlockSpec((grp, SB_HEADS, SUBLANES, SB_HEAD_DIM), lambda s, pt: (s, 0, 0, 0))
    kernel = functools.partial(_sb_decode_kernel, n_pages=n_pages, n_new=n_new)
    return pl.pallas_call(
        kernel,
        grid_spec=pltpu.PrefetchScalarGridSpec(
            num_scalar_prefetch=1,
            grid=(nbatch // grp,),
            in_specs=[blk, any_spec, any_spec, any_spec, any_spec],
            out_specs=blk,
            scratch_shapes=[
                pltpu.VMEM((2 * grp, SB_HEADS, SB_HEAD_DIM, ntok), F32),
                pltpu.VMEM((2 * grp, SB_HEADS, SB_HEAD_DIM, ntok), F32),
                pltpu.VMEM((1, SB_HEADS, SB_HEAD_DIM, SBD_PAGES * page), F32),
                pltpu.VMEM((1, SB_HEADS, SB_HEAD_DIM, SBD_PAGES * page), F32),
                pltpu.SemaphoreType.DMA((2, 2)),
                pltpu.SemaphoreType.DMA((2,)),
            ]),
        out_shape=jax.ShapeDtypeStruct((nbatch, SB_HEADS, SUBLANES, SB_HEAD_DIM), F32),
        compiler_params=pltpu.CompilerParams(
            dimension_semantics=("arbitrary",), vmem_limit_bytes=VMEM_LIMIT_BYTES),
        name="sb_decode",
    )(pt_flat, q8, knew_t, vnew_t, cache_kt, cache_vt)


def _mla_decode_kernel(pt_ref, q_ref, ckvnew_ref, kpenew_ref, ckv_hbm, kpe_hbm, o_ref,
                       ckvbuf, kpebuf, m_ref, l_ref, acc_ref, sem, *, n_pages):
    b = pl.program_id(0)
    c = pl.program_id(1)
    n_chunks = pl.num_programs(1)
    n = b * n_chunks + c
    total = pl.num_programs(0) * n_chunks
    page = ckv_hbm.shape[1]

    def copies(i, slot):
        bb = i // n_chunks
        cc = i % n_chunks
        out = []
        for j in range(MLAD_PAGES):
            pid = pt_ref[bb * n_pages + cc * MLAD_PAGES + j]
            out.append(pltpu.make_async_copy(
                ckv_hbm.at[pid], ckvbuf.at[slot, pl.ds(j * page, page), :], sem.at[slot, 0]))
            out.append(pltpu.make_async_copy(
                kpe_hbm.at[pid], kpebuf.at[slot, :, pl.ds(j * page, page)], sem.at[slot, 1]))
        return out

    @pl.when(n == 0)
    def _():
        for cp in copies(0, 0):
            cp.start()

    slot = n % 2

    @pl.when(n + 1 < total)
    def _():
        for cp in copies(n + 1, 1 - slot):
            cp.start()

    q = q_ref[0]
    q_lat = q[:, :KV_LORA]
    q_pe = q[:, KV_LORA:KV_LORA + MLA_ROPE_DIM]

    def scores(ckv, kpe_t):
        return (_dot_nt(q_lat, ckv) + _dot(q_pe, kpe_t)) * MLA_SCALE

    @pl.when(c == 0)
    def _():
        ckv = ckvnew_ref[0].astype(BF16)
        s = scores(ckv, kpenew_ref[0].astype(BF16))
        r_iota = lax.broadcasted_iota(jnp.int32, s.shape, 0)
        c_iota = lax.broadcasted_iota(jnp.int32, s.shape, 1)
        s = jnp.where(c_iota <= r_iota // MLA_HEADS, s, NEG_INF)
        m = jnp.max(s, axis=-1, keepdims=True)
        p = jnp.exp(s - m)
        m_ref[...] = m
        l_ref[...] = jnp.sum(p, axis=-1, keepdims=True)
        acc_ref[...] = _dot(p.astype(BF16), ckv)

    for cp in copies(n, slot):
        cp.wait()

    ckv = ckvbuf[slot].astype(BF16)
    s = scores(ckv, kpebuf[slot].astype(BF16))
    m_old = m_ref[...]
    m_new = jnp.maximum(m_old, jnp.max(s, axis=-1, keepdims=True))
    p = jnp.exp(s - m_new)
    alpha = jnp.exp(m_old - m_new)
    l_new = alpha * l_ref[...] + jnp.sum(p, axis=-1, keepdims=True)
    acc_new = alpha * acc_ref[...] + _dot(p.astype(BF16), ckv)
    m_ref[...] = m_new
    l_ref[...] = l_new
    acc_ref[...] = acc_new

    @pl.when(c == n_chunks - 1)
    def _():
        o_ref[0] = acc_new / l_new


def _mla_decode_call(pt_flat, qcs, ckv_new, kpe_new_t, cache_ckv, cache_kpe_t, n_pages):
    nbatch, rows, _ = qcs.shape
    page = cache_ckv.shape[1]
    chunk = MLAD_PAGES * page
    any_spec = pl.BlockSpec(memory_space=pl.ANY)
    per_b = lambda shape: pl.BlockSpec((1,) + shape, lambda b, c, pt: (b, 0, 0))
    kernel = functools.partial(_mla_decode_kernel, n_pages=n_pages)
    return pl.pallas_call(
        kernel,
        grid_spec=pltpu.PrefetchScalarGridSpec(
            num_scalar_prefetch=1,
            grid=(nbatch, n_pages // MLAD_PAGES),
            in_specs=[per_b((rows, QCAT)), per_b((page, KV_LORA)), per_b((MLA_ROPE_DIM, page)),
                      any_spec, any_spec],
            out_specs=per_b((rows, KV_LORA)),
            scratch_shapes=[
                pltpu.VMEM((2, chunk, KV_LORA), F32),
                pltpu.VMEM((2, MLA_ROPE_DIM, chunk), F32),
                pltpu.VMEM((rows, 1), F32),
                pltpu.VMEM((rows, 1), F32),
                pltpu.VMEM((rows, KV_LORA), F32),
                pltpu.SemaphoreType.DMA((2, 2)),
            ]),
        out_shape=jax.ShapeDtypeStruct((nbatch, rows, KV_LORA), F32),
        compiler_params=pltpu.CompilerParams(
            dimension_semantics=("arbitrary", "arbitrary"), vmem_limit_bytes=VMEM_LIMIT_BYTES),
        name="mla_decode",
    )(pt_flat, qcs, ckv_new, kpe_new_t, cache_ckv, cache_kpe_t)


def _prep_ffn(g_pre, w_gate, w_up, w_down, g_post):
    d, d_ff = w_gate.shape
    n = d_ff // FFN_CHUNK
    wg = w_gate.astype(BF16).reshape(d, n, FFN_CHUNK)
    wu = w_up.astype(BF16).reshape(d, n, FFN_CHUNK)
    wgu = jnp.concatenate([wg, wu], axis=-1).transpose(1, 0, 2)
    wd = w_down.astype(BF16).reshape(n, FFN_CHUNK, d)
    return g_pre.reshape(1, d), wgu, wd, g_post.reshape(1, d)


def _prep_proj(g_mix_pre, w_in, g_q_lora, w_q_up, w_uk, g_kv_lora):
    d = w_in.shape[0]
    half = MLA_ROPE_DIM // 2
    kpe_w = w_in[:, _IN_KPE:_IN_KPE + MLA_ROPE_DIM]
    pad = jnp.zeros((d, LANES - MLA_ROPE_DIM), w_in.dtype)
    w_in_ext = jnp.concatenate([
        w_in[:, :SB_WIDTH] * SB_SCALE, w_in[:, SB_WIDTH:_IN_KPE],
        kpe_w, pad, kpe_w[:, half:], kpe_w[:, :half], pad], axis=1).astype(BF16)

    wq = w_q_up.reshape(Q_LORA, MLA_HEADS, MLA_QK_DIM)
    nope = wq[:, :, :MLA_NOPE_DIM].reshape(Q_LORA, MLA_HEADS * MLA_NOPE_DIM)
    pe = wq[:, :, MLA_NOPE_DIM:]
    pe_swap = jnp.concatenate([pe[:, :, half:], pe[:, :, :half]], axis=-1)
    lane_pad = lambda a: jnp.pad(a, ((0, 0), (0, 0), (0, LANES - MLA_ROPE_DIM))).reshape(Q_LORA, MLA_HEADS * LANES)
    w_qup_ext = jnp.concatenate([nope, lane_pad(pe), lane_pad(pe_swap)], axis=1).astype(BF16)

    eye = jnp.eye(MLA_HEADS, dtype=w_uk.dtype)
    w_uk_bd = jnp.einsum('chn,hg->hngc', w_uk, eye).reshape(MLA_HEADS * MLA_NOPE_DIM, MLA_HEADS * KV_LORA)
    return (g_mix_pre.reshape(1, d), w_in_ext, g_q_lora.reshape(1, Q_LORA), w_qup_ext,
            w_uk_bd.astype(BF16), g_kv_lora.reshape(1, KV_LORA))


def _prep_out(g_sb_out, g_mla_out, w_uv, w_out, g_mix_post):
    eye = jnp.eye(MLA_HEADS, dtype=w_uv.dtype)
    w_uv_bd = jnp.einsum('chv,hg->hcgv', w_uv, eye).reshape(MLA_HEADS * KV_LORA, MLA_WIDTH)
    return (g_sb_out.reshape(1, SB_WIDTH), g_mla_out.reshape(1, MLA_WIDTH), w_uv_bd.astype(BF16),
            w_out.astype(BF16), g_mix_post.reshape(1, -1))


def _rope_tables(pos):
    half = MLA_ROPE_DIM // 2
    inv_freq = ROPE_THETA ** (-jnp.arange(half, dtype=F32) / half)
    ang = pos.astype(F32)[:, None] * inv_freq[None, :]
    cos, sin = jnp.cos(ang), jnp.sin(ang)
    pad = jnp.zeros((pos.shape[0], LANES - MLA_ROPE_DIM), F32)
    return (jnp.concatenate([cos, cos, pad], axis=1), jnp.concatenate([-sin, sin, pad], axis=1))


def _token_tile(t):
    for tm in (256, 128):
        if t % tm == 0:
            return tm
    raise ValueError(f"token count {t} is not a multiple of 128")


def kernel(x_prompt, x_sample, cache_sb_k, cache_sb_v, cache_mla_ckv, cache_mla_kpe, page_table,
           g_ffn1_pre, w_ffn1_gate, w_ffn1_up, w_ffn1_down, g_ffn1_post,
           g_mix_pre, w_in, g_q_lora, w_q_up, g_kv_lora, w_uk, w_uv, g_sb_out, g_mla_out, w_out, g_mix_post,
           g_ffn2_pre, w_ffn2_gate, w_ffn2_up, w_ffn2_down, g_ffn2_post):
    batch, seq, d = x_prompt.shape
    dec_batch, dec_seq, _ = x_sample.shape
    page = cache_sb_k.shape[1]
    n_pages = page_table.shape[1]
    past_len = n_pages * page
    assert seq % ATTN_BLOCK == 0 and dec_seq <= SUBLANES
    assert n_pages % MLAD_PAGES == 0 and n_pages % SBD_PAGES == 0 and n_pages >= 2 * SBD_PAGES
    assert dec_batch % SBD_GROUP == 0

    ffn1 = _prep_ffn(g_ffn1_pre, w_ffn1_gate, w_ffn1_up, w_ffn1_down, g_ffn1_post)
    ffn2 = _prep_ffn(g_ffn2_pre, w_ffn2_gate, w_ffn2_up, w_ffn2_down, g_ffn2_post)
    proj_w = _prep_proj(g_mix_pre, w_in, g_q_lora, w_q_up, w_uk, g_kv_lora)
    out_w = _prep_out(g_sb_out, g_mla_out, w_uv, w_out, g_mix_post)

    cos_p, sin_p = _rope_tables(jnp.arange(seq))
    tm_p = _token_tile(seq)
    (x1_p, sbk_p, sbv_p, ckv_p, kpe_p, qa_p, ka_p, va_p, qcat_p, kcat_p) = _ffn_proj_call(
        x_prompt, cos_p, sin_p, ffn1, proj_w, tm_p)
    osb_p, olat_p = _prompt_attn_call(qa_p, ka_p, va_p, qcat_p, kcat_p)
    y_p = _out_ffn_call(osb_p, olat_p, x1_p, out_w, ffn2, tm_p)

    n_tok = dec_batch * dec_seq
    cos_s, sin_s = _rope_tables(past_len + (jnp.arange(n_tok) % dec_seq))
    tm_s = _token_tile(n_tok)
    (x1_s, sbk_s, sbv_s, ckv_s, kpe_s, qa_s, _, _, qcat_s, _) = _ffn_proj_call(
        x_sample.reshape(1, n_tok, d), cos_s, sin_s, ffn1, proj_w, tm_s)

    pt_flat = page_table.reshape(-1)
    pad_tok = page - dec_seq
    q8 = qa_s[0].transpose(1, 0, 2).reshape(dec_batch, dec_seq, SB_HEADS, SB_HEAD_DIM).transpose(0, 2, 1, 3)
    q8 = jnp.pad(q8, ((0, 0), (0, 0), (0, SUBLANES - dec_seq), (0, 0))).astype(F32)
    head_dim_tok = lambda a: jnp.pad(
        a.reshape(dec_batch, dec_seq, SB_HEADS, SB_HEAD_DIM).transpose(0, 2, 3, 1),
        ((0, 0), (0, 0), (0, 0), (0, pad_tok)))
    o8 = _sb_decode_call(pt_flat, q8, head_dim_tok(sbk_s), head_dim_tok(sbv_s),
                         cache_sb_k.transpose(0, 2, 3, 1), cache_sb_v.transpose(0, 2, 3, 1), n_pages, dec_seq)
    osb_s = o8[:, :, :dec_seq, :].transpose(0, 2, 1, 3).reshape(n_tok, SB_PAIRS, LANES).transpose(1, 0, 2)[None]

    qcs = qcat_s[0].reshape(MLA_HEADS, dec_batch, dec_seq, QCAT).transpose(1, 2, 0, 3)
    qcs = qcs.reshape(dec_batch, dec_seq * MLA_HEADS, QCAT)
    ckv_new = jnp.pad(ckv_s.reshape(dec_batch, dec_seq, KV_LORA), ((0, 0), (0, pad_tok), (0, 0)))
    kpe_new_t = jnp.pad(kpe_s.reshape(dec_batch, dec_seq, MLA_ROPE_DIM).transpose(0, 2, 1),
                        ((0, 0), (0, 0), (0, pad_tok)))
    olat = _mla_decode_call(pt_flat, qcs, ckv_new, kpe_new_t, cache_mla_ckv,
                            cache_mla_kpe.transpose(0, 2, 1), n_pages)
    olat_s = olat.reshape(dec_batch, dec_seq, MLA_HEADS, KV_LORA).transpose(2, 0, 1, 3)
    olat_s = olat_s.reshape(1, MLA_HEADS, n_tok, KV_LORA)

    y_s = _out_ffn_call(osb_s, olat_s, x1_s, out_w, ffn2, tm_s)

    return (y_p, y_s.reshape(dec_batch, dec_seq, d),
            sbk_p.reshape(batch, seq, SB_HEADS, SB_HEAD_DIM), sbv_p.reshape(batch, seq, SB_HEADS, SB_HEAD_DIM),
            ckv_p, kpe_p,
            sbk_s.reshape(dec_batch, dec_seq, SB_HEADS, SB_HEAD_DIM),
            sbv_s.reshape(dec_batch, dec_seq, SB_HEADS, SB_HEAD_DIM),
            ckv_s.reshape(dec_batch, dec_seq, KV_LORA), kpe_s.reshape(dec_batch, dec_seq, MLA_ROPE_DIM))
```

```python
import functools

import jax
import jax.numpy as jnp
from jax import lax
from jax.experimental import pallas as pl
from jax.experimental.pallas import tpu as pltpu

F32 = jnp.float32
BF16 = jnp.bfloat16

SB_HEADS = 8
SB_HEAD_DIM = 64
SB_WIDTH = SB_HEADS * SB_HEAD_DIM
SB_PAIRS = SB_HEADS // 2
MLA_HEADS = 8
MLA_NOPE_DIM = 64
MLA_ROPE_DIM = 32
MLA_V_DIM = 64
MLA_QK_DIM = MLA_NOPE_DIM + MLA_ROPE_DIM
MLA_WIDTH = MLA_HEADS * MLA_V_DIM
Q_LORA = 256
KV_LORA = 128
FFN_RESIDUAL_WEIGHT = 0.5
ROPE_THETA = 10000.0
NORM_EPS = 1e-6
SB_SCALE = SB_HEAD_DIM ** -0.5
MLA_SCALE = MLA_QK_DIM ** -0.5
NEG_INF = -1e30

LANES = 128
SUBLANES = 8
MXU_DIM = 256
VMEM_LIMIT_BYTES = 56 * 1024 * 1024

FFN_CHUNK = MXU_DIM
ATTN_BLOCK = MXU_DIM
MLA_CHUNK = 2 * MXU_DIM
QCAT = 2 * LANES
SB_EXIT = 104.0

SBD_GROUP = 4
SBD_PAGES = 2
MLAD_PAGES = 64


def _dot(a, b):
    return jnp.dot(a, b, preferred_element_type=F32)


def _dot_nt(a, b):
    return lax.dot_general(a, b, (((1,), (1,)), ((), ())), preferred_element_type=F32)


def _rms(x, g):
    ms = jnp.mean(x * x, axis=-1, keepdims=True)
    return x * lax.rsqrt(ms + NORM_EPS) * g


def _softplus(z):
    return jnp.maximum(z, 0.0) + jnp.log(1.0 + jnp.exp(-jnp.abs(z)))


def _split_bf16(x):
    hi = x.astype(BF16)
    lo = (x - hi.astype(F32)).astype(BF16)
    return hi, lo


def _suffix_sums(x, tri):
    hi, lo = _split_bf16(x)
    return _dot(hi, tri) + _dot(lo, tri)


def _tri(n):
    row = lax.broadcasted_iota(jnp.int32, (n, n), 0)
    col = lax.broadcasted_iota(jnp.int32, (n, n), 1)
    return (row >= col).astype(BF16)


def _const_spec(shape):
    zeros = (0,) * len(shape)
    return pl.BlockSpec(shape, lambda *_: zeros, pipeline_mode=pl.Buffered(1))


def _half_step_ffn(x, ffn_refs, act_ref):
    g_pre_ref, wg_ref, wu_ref, wd_ref, g_post_ref = ffn_refs
    h = _rms(x, g_pre_ref[...]).astype(BF16)
    for c in range(wg_ref.shape[1] // FFN_CHUNK):
        cols = slice(c * FFN_CHUNK, (c + 1) * FFN_CHUNK)
        g = _dot(h, wg_ref[:, cols])
        u = _dot(h, wu_ref[:, cols])
        act_ref[:, cols] = ((g * jax.nn.sigmoid(g)) * u).astype(BF16)
    y = _dot(act_ref[...], wd_ref[...])
    return x + FFN_RESIDUAL_WEIGHT * _rms(y, g_post_ref[...])


_IN_Q = 0
_IN_K = SB_WIDTH
_IN_V = 2 * SB_WIDTH
_IN_QD = 3 * SB_WIDTH
_IN_KVD = _IN_QD + Q_LORA
_IN_KPE = _IN_KVD + KV_LORA
_IN_KPE_SWAP = _IN_KPE + LANES
_IN_COLS_EXT = _IN_KPE_SWAP + LANES
_QUP_NOPE = 0
_QUP_PE = MLA_HEADS * MLA_NOPE_DIM
_QUP_PE_SWAP = _QUP_PE + MLA_HEADS * LANES
_QUP_COLS_EXT = _QUP_PE_SWAP + MLA_HEADS * LANES


def _ffn_proj_kernel(x_ref, cos_ref, sin_ref, g_pre_ref, wg_ref, wu_ref, wd_ref, g_post_ref,
                     g_mix_ref, w_in_ref, g_q_ref, w_qup_ref, w_uk_ref, g_kv_ref,
                     x1_ref, sbk_ref, sbv_ref, ckv_ref, kpe_ref,
                     qa_ref, ka_ref, va_ref, qcat_ref, kcat_ref, act_ref):
    x1 = _half_step_ffn(x_ref[0], (g_pre_ref, wg_ref, wu_ref, wd_ref, g_post_ref), act_ref)
    x1_ref[0] = x1

    hm = _rms(x1, g_mix_ref[...]).astype(BF16)
    proj = _dot(hm, w_in_ref[...])
    sbk = proj[:, _IN_K:_IN_K + SB_WIDTH]
    sbv = proj[:, _IN_V:_IN_V + SB_WIDTH]
    sbk_ref[0] = sbk
    sbv_ref[0] = sbv
    for p in range(SB_PAIRS):
        lo, hi = p * LANES, (p + 1) * LANES
        qa_ref[0, p] = proj[:, _IN_Q + lo:_IN_Q + hi].astype(BF16)
        ka_ref[0, p] = sbk[:, lo:hi].astype(BF16)
        va_ref[0, p] = sbv[:, lo:hi].astype(BF16)

    cos = cos_ref[...]
    sin = sin_ref[...]
    qn = _rms(proj[:, _IN_QD:_IN_QD + Q_LORA], g_q_ref[...]).astype(BF16)
    qup = _dot(qn, w_qup_ref[...])
    qlat = _dot(qup[:, _QUP_NOPE:_QUP_PE].astype(BF16), w_uk_ref[...])
    for h in range(MLA_HEADS):
        lo, hi = h * LANES, (h + 1) * LANES
        pe = (qup[:, _QUP_PE + lo:_QUP_PE + hi] * cos
              + qup[:, _QUP_PE_SWAP + lo:_QUP_PE_SWAP + hi] * sin)
        qcat_ref[0, h, :, :LANES] = qlat[:, lo:hi].astype(BF16)
        qcat_ref[0, h, :, LANES:] = pe.astype(BF16)

    ckv = _rms(proj[:, _IN_KVD:_IN_KVD + KV_LORA], g_kv_ref[...])
    kpe = (proj[:, _IN_KPE:_IN_KPE + LANES] * cos
           + proj[:, _IN_KPE_SWAP:_IN_KPE_SWAP + LANES] * sin)
    ckv_ref[0] = ckv
    kpe_ref[0] = kpe[:, :MLA_ROPE_DIM]
    kcat_ref[0, :, :LANES] = ckv.astype(BF16)
    kcat_ref[0, :, LANES:] = kpe.astype(BF16)


def _ffn_proj_call(x, cos_t, sin_t, ffn_w, proj_w, tm):
    nb, t, d = x.shape
    weights = tuple(ffn_w) + tuple(proj_w)
    d_ff = ffn_w[1].shape[1]
    tok = lambda shape: pl.BlockSpec((1, tm) + shape, lambda b, i: (b, i) + (0,) * len(shape))
    heads = lambda n, w: pl.BlockSpec((1, n, tm, w), lambda b, i: (b, 0, i, 0))
    tab = pl.BlockSpec((tm, LANES), lambda b, i: (i, 0))
    in_specs = [tok((d,)), tab, tab] + [_const_spec(w.shape) for w in weights]
    out_shape = (
        jax.ShapeDtypeStruct((nb, t, d), F32),
        jax.ShapeDtypeStruct((nb, t, SB_WIDTH), F32),
        jax.ShapeDtypeStruct((nb, t, SB_WIDTH), F32),
        jax.ShapeDtypeStruct((nb, t, KV_LORA), F32),
        jax.ShapeDtypeStruct((nb, t, MLA_ROPE_DIM), F32),
        jax.ShapeDtypeStruct((nb, SB_PAIRS, t, LANES), BF16),
        jax.ShapeDtypeStruct((nb, SB_PAIRS, t, LANES), BF16),
        jax.ShapeDtypeStruct((nb, SB_PAIRS, t, LANES), BF16),
        jax.ShapeDtypeStruct((nb, MLA_HEADS, t, QCAT), BF16),
        jax.ShapeDtypeStruct((nb, t, QCAT), BF16),
    )
    out_specs = (tok((d,)), tok((SB_WIDTH,)), tok((SB_WIDTH,)), tok((KV_LORA,)), tok((MLA_ROPE_DIM,)),
                 heads(SB_PAIRS, LANES), heads(SB_PAIRS, LANES), heads(SB_PAIRS, LANES),
                 heads(MLA_HEADS, QCAT), tok((QCAT,)))
    return pl.pallas_call(
        _ffn_proj_kernel,
        grid=(nb, t // tm),
        in_specs=in_specs,
        out_specs=out_specs,
        out_shape=out_shape,
        scratch_shapes=[pltpu.VMEM((tm, d_ff), BF16)],
        compiler_params=pltpu.CompilerParams(
            dimension_semantics=("parallel", "parallel"), vmem_limit_bytes=VMEM_LIMIT_BYTES),
        name="ffn_proj",
    )(x, cos_t, sin_t, *weights)


def _out_ffn_kernel(osb_ref, olat_ref, x1_ref, g_sb_ref, g_mla_ref, w_uv_ref, w_out_ref, g_mixpost_ref,
                    g_pre_ref, wg_ref, wu_ref, wd_ref, g_post_ref, y_ref, act_ref):
    osb = jnp.concatenate([osb_ref[0, p] for p in range(SB_PAIRS)], axis=-1)
    olat = jnp.concatenate([olat_ref[0, h] for h in range(MLA_HEADS)], axis=-1)
    omla = _dot(olat.astype(BF16), w_uv_ref[...])
    merged = jnp.concatenate([_rms(osb, g_sb_ref[...]), _rms(omla, g_mla_ref[...])], axis=-1)
    o = _dot(merged.astype(BF16), w_out_ref[...])
    x2 = x1_ref[0] + _rms(o, g_mixpost_ref[...])
    y_ref[0] = _half_step_ffn(x2, (g_pre_ref, wg_ref, wu_ref, wd_ref, g_post_ref), act_ref)


def _out_ffn_call(osb, olat, x1, out_w, ffn_w, tm):
    nb, t, d = x1.shape
    d_ff = ffn_w[1].shape[1]
    heads = lambda n: pl.BlockSpec((1, n, tm, LANES), lambda b, i: (b, 0, i, 0))
    tok = pl.BlockSpec((1, tm, d), lambda b, i: (b, i, 0))
    weights = tuple(out_w) + tuple(ffn_w)
    return pl.pallas_call(
        _out_ffn_kernel,
        grid=(nb, t // tm),
        in_specs=[heads(SB_PAIRS), heads(MLA_HEADS), tok] + [_const_spec(w.shape) for w in weights],
        out_specs=tok,
        out_shape=jax.ShapeDtypeStruct((nb, t, d), F32),
        scratch_shapes=[pltpu.VMEM((tm, d_ff), BF16)],
        compiler_params=pltpu.CompilerParams(
            dimension_semantics=("parallel", "parallel"), vmem_limit_bytes=VMEM_LIMIT_BYTES),
        name="out_ffn",
    )(osb, olat, x1, *weights)


def _sb_pair_block(q2, k, v2, tri, carry, acc, valid):
    blk = k.shape[0]
    z = _dot_nt(q2, k)
    sp = _softplus(z)
    if valid is not None:
        sp = jnp.where(valid, sp, 0.0)
    suf = _suffix_sums(sp, tri)
    a = jnp.exp(z - suf - carry)
    if valid is not None:
        a = jnp.where(valid, a, 0.0)
    a = a.astype(BF16)
    acc = acc + _dot(jnp.concatenate([a[:blk], a[blk:]], axis=1), v2)
    return carry + suf[:, 0:1], acc


def _prompt_attn_kernel(qa_ref, ka_ref, va_ref, qcat_ref, kcat_ref, osb_ref, olat_ref):
    qi = pl.program_id(1)
    blk = ATTN_BLOCK
    tri = _tri(blk)
    row2 = lax.broadcasted_iota(jnp.int32, (2 * blk, blk), 0)
    col2 = lax.broadcasted_iota(jnp.int32, (2 * blk, blk), 1)
    strict = col2 < jnp.where(row2 >= blk, row2 - blk, row2)
    even_q = jnp.logical_xor(lax.broadcasted_iota(jnp.int32, (2 * blk, LANES), 1) >= SB_HEAD_DIM,
                             lax.broadcasted_iota(jnp.int32, (2 * blk, LANES), 0) < blk)
    diag = pl.multiple_of(qi * blk, blk)

    def pair_operands(p, off):
        k = ka_ref[0, p, pl.ds(off, blk), :]
        v = va_ref[0, p, pl.ds(off, blk), :]
        v2 = jnp.concatenate([v, v], axis=0)
        return k, jnp.where(even_q, v2, jnp.zeros_like(v2))

    q2s = []
    for p in range(SB_PAIRS):
        qp = qa_ref[0, p]
        qp2 = jnp.concatenate([qp, qp], axis=0)
        q2s.append(jnp.where(even_q, qp2, jnp.zeros_like(qp2)))

    def sb_step(off, carries, accs, valid):
        out_c, out_a = [], []
        for p in range(SB_PAIRS):
            k, v2 = pair_operands(p, off)
            c, a = _sb_pair_block(q2s[p], k, v2, tri, carries[p], accs[p], valid)
            out_c.append(c)
            out_a.append(a)
        return tuple(out_c), tuple(out_a)

    c0 = tuple(jnp.zeros((2 * blk, 1), F32) for _ in range(SB_PAIRS))
    a0 = tuple(jnp.zeros((blk, LANES), F32) for _ in range(SB_PAIRS))
    carries, accs = sb_step(diag, c0, a0, strict)

    def sb_cond(st):
        j, carries, _ = st
        least = functools.reduce(jnp.minimum, carries)
        return jnp.logical_and(j >= 0, jnp.min(least) < SB_EXIT)

    def sb_body(st):
        j, carries, accs = st
        carries, accs = sb_step(pl.multiple_of(j * blk, blk), carries, accs, None)
        return j - 1, carries, accs

    _, _, accs = lax.while_loop(sb_cond, sb_body, (qi - 1, carries, accs))
    for p in range(SB_PAIRS):
        osb_ref[0, p] = accs[p]

    rows = MLA_HEADS * blk
    ch = MLA_CHUNK
    qc = qcat_ref[0].reshape(rows, QCAT)

    def mla_chunk(off, m, l, acc, masked):
        kc = kcat_ref[0, pl.ds(off, ch), :]
        s = _dot_nt(qc, kc) * MLA_SCALE
        if masked:
            q_pos = diag + lax.broadcasted_iota(jnp.int32, (1, blk, ch), 1)
            k_pos = off + lax.broadcasted_iota(jnp.int32, (1, blk, ch), 2)
            s = jnp.where(k_pos <= q_pos, s.reshape(MLA_HEADS, blk, ch), NEG_INF).reshape(rows, ch)
        m_new = jnp.maximum(m, jnp.max(s, axis=-1, keepdims=True))
        p = jnp.exp(s - m_new)
        alpha = jnp.exp(m - m_new)
        l = alpha * l + jnp.sum(p, axis=-1, keepdims=True)
        acc = alpha * acc + _dot(p.astype(BF16), kc[:, :KV_LORA])
        return m_new, l, acc

    n_full = diag // ch

    def mla_body(j, st):
        return mla_chunk(pl.multiple_of(j * ch, ch), *st, False)

    st = (jnp.full((rows, 1), NEG_INF, F32), jnp.zeros((rows, 1), F32), jnp.zeros((rows, KV_LORA), F32))
    st = lax.fori_loop(0, n_full, mla_body, st)
    _, l, acc = mla_chunk(pl.multiple_of(n_full * ch, ch), *st, True)
    olat_ref[0] = (acc / l).reshape(MLA_HEADS, blk, KV_LORA)


def _prompt_attn_call(qa, ka, va, qcat, kcat):
    nb, _, t, _ = qa.shape
    blk = ATTN_BLOCK
    q_spec = lambda n, w: pl.BlockSpec((1, n, blk, w), lambda b, i: (b, 0, i, 0))
    kv_spec = pl.BlockSpec((1, SB_PAIRS, t, LANES), lambda b, i: (b, 0, 0, 0))
    return pl.pallas_call(
        _prompt_attn_kernel,
        grid=(nb, t // blk),
        in_specs=[q_spec(SB_PAIRS, LANES), kv_spec, kv_spec, q_spec(MLA_HEADS, QCAT),
                  pl.BlockSpec((1, t, QCAT), lambda b, i: (b, 0, 0))],
        out_specs=(q_spec(SB_PAIRS, LANES), q_spec(MLA_HEADS, KV_LORA)),
        out_shape=(jax.ShapeDtypeStruct((nb, SB_PAIRS, t, LANES), F32),
                   jax.ShapeDtypeStruct((nb, MLA_HEADS, t, KV_LORA), F32)),
        compiler_params=pltpu.CompilerParams(
            dimension_semantics=("parallel", "parallel"), vmem_limit_bytes=VMEM_LIMIT_BYTES),
        name="prompt_attn",
    )(qa, ka, va, qcat, kcat)


def _sb_decode_kernel(pt_ref, q_ref, knew_hbm, vnew_hbm, kc_hbm, vc_hbm, o_ref,
                      kbuf, vbuf, kslow, vslow, sem, slow_sem, *, n_pages, n_new):
    step = pl.program_id(0)
    n_steps = pl.num_programs(0)
    grp = q_ref.shape[0]
    page = kc_hbm.shape[-1]
    first = SBD_PAGES * page
    ntok = first + page

    def first_copies(s, slot):
        out = []
        for g in range(grp):
            b = s * grp + g
            for hbm, new, buf, t in ((kc_hbm, knew_hbm, kbuf, 0), (vc_hbm, vnew_hbm, vbuf, 1)):
                for j in range(SBD_PAGES):
                    pid = pt_ref[b * n_pages + (n_pages - SBD_PAGES + j)]
                    out.append(pltpu.make_async_copy(
                        hbm.at[pid], buf.at[slot * grp + g, :, :, pl.ds(j * page, page)], sem.at[slot, t]))
                out.append(pltpu.make_async_copy(
                    new.at[b], buf.at[slot * grp + g, :, :, pl.ds(first, page)], sem.at[slot, t]))
        return out

    @pl.when(step == 0)
    def _():
        for c in first_copies(0, 0):
            c.start()

    slot = step % 2

    @pl.when(step + 1 < n_steps)
    def _():
        for c in first_copies(step + 1, 1 - slot):
            c.start()

    for c in first_copies(step, slot):
        c.wait()

    rows = SB_HEADS * SUBLANES
    r_iota = lax.broadcasted_iota(jnp.int32, (rows, ntok), 0)
    c_iota = lax.broadcasted_iota(jnp.int32, (rows, ntok), 1)
    valid = jnp.logical_or(c_iota < first, (c_iota - first) < (r_iota % SUBLANES))
    real_row = (lax.broadcasted_iota(jnp.int32, (rows, 1), 0) % SUBLANES) < n_new
    tri = _tri(ntok)
    tri_slow = _tri(first)

    def scores(qs, buf, idx):
        return jnp.concatenate([_dot(qs[h], buf[idx, h].astype(BF16)) for h in range(SB_HEADS)], axis=0)

    def weighted(a, buf, idx):
        return jnp.concatenate(
            [_dot_nt(a[h * SUBLANES:(h + 1) * SUBLANES], buf[idx, h].astype(BF16)) for h in range(SB_HEADS)],
            axis=0)

    def newest_block(g):
        idx = slot * grp + g
        qs = [q_ref[g, h].astype(BF16) for h in range(SB_HEADS)]
        z = scores(qs, kbuf, idx)
        sp = jnp.where(valid, _softplus(z), 0.0)
        suf = _suffix_sums(sp, tri)
        a = jnp.where(valid, jnp.exp(z - suf), 0.0).astype(BF16)
        return qs, suf[:, 0:1], weighted(a, vbuf, idx)

    newest = [newest_block(g) for g in range(grp)]

    for g in range(grp):
        b = step * grp + g
        qs, carry, acc = newest[g]

        def cond(st):
            c, carry, _ = st
            live = jnp.min(jnp.where(real_row, carry, SB_EXIT)) < SB_EXIT
            return jnp.logical_and(c >= 0, live)

        def body(st):
            c, carry, acc = st
            copies = []
            for hbm, buf, t in ((kc_hbm, kslow, 0), (vc_hbm, vslow, 1)):
                for j in range(SBD_PAGES):
                    pid = pt_ref[b * n_pages + c * SBD_PAGES + j]
                    copies.append(pltpu.make_async_copy(
                        hbm.at[pid], buf.at[0, :, :, pl.ds(j * page, page)], slow_sem.at[t]))
            for cp in copies:
                cp.start()
            for cp in copies:
                cp.wait()
            z = scores(qs, kslow, 0)
            suf = _suffix_sums(_softplus(z), tri_slow)
            a = jnp.exp(z - suf - carry).astype(BF16)
            return c - 1, carry + suf[:, 0:1], acc + weighted(a, vslow, 0)

        _, _, acc = lax.while_loop(cond, body, (n_pages // SBD_PAGES - 2, carry, acc))
        o_ref[g] = acc.reshape(SB_HEADS, SUBLANES, SB_HEAD_DIM)


def _sb_decode_call(pt_flat, q8, knew_t, vnew_t, cache_kt, cache_vt, n_pages, n_new):
    nbatch = q8.shape[0]
    page = cache_kt.shape[-1]
    grp = SBD_GROUP
    ntok = (SBD_PAGES + 1) * page
    any_spec = pl.BlockSpec(memory_space=pl.ANY)
    blk = pl.BlockSpec((grp, SB_HEADS, SUBLANES, SB_HEAD_DIM), lambda s, pt: (s, 0, 0, 0))
    kernel = functools.partial(_sb_decode_kernel, n_pages=n_pages, n_new=n_new)
    return pl.pallas_call(
        kernel,
        grid_spec=pltpu.PrefetchScalarGridSpec(
            num_scalar_prefetch=1,
            grid=(nbatch // grp,),
            in_specs=[blk, any_spec, any_spec, any_spec, any_spec],
            out_specs=blk,
            scratch_shapes=[
                pltpu.VMEM((2 * grp, SB_HEADS, SB_HEAD_DIM, ntok), F32),
                pltpu.VMEM((2 * grp, SB_HEADS, SB_HEAD_DIM, ntok), F32),
                pltpu.VMEM((1, SB_HEADS, SB_HEAD_DIM, SBD_PAGES * page), F32),
                pltpu.VMEM((1, SB_HEADS, SB_HEAD_DIM, SBD_PAGES * page), F32),
                pltpu.SemaphoreType.DMA((2, 2)),
                pltpu.SemaphoreType.DMA((2,)),
            ]),
        out_shape=jax.ShapeDtypeStruct((nbatch, SB_HEADS, SUBLANES, SB_HEAD_DIM), F32),
        compiler_params=pltpu.CompilerParams(
            dimension_semantics=("arbitrary",), vmem_limit_bytes=VMEM_LIMIT_BYTES),
        name="sb_decode",
    )(pt_flat, q8, knew_t, vnew_t, cache_kt, cache_vt)


def _mla_decode_kernel(pt_ref, q_ref, ckvnew_ref, kpenew_ref, ckv_hbm, kpe_hbm, o_ref,
                       ckvbuf, kpebuf, m_ref, l_ref, acc_ref, sem, *, n_pages):
    b = pl.program_id(0)
    c = pl.program_id(1)
    n_chunks = pl.num_programs(1)
    n = b * n_chunks + c
    total = pl.num_programs(0) * n_chunks
    page = ckv_hbm.shape[1]
    chunk_pages = ckvbuf.shape[1] // page

    def copies(i, slot):
        bb = i // n_chunks
        cc = i % n_chunks
        out = []
        for j in range(chunk_pages):
            pid = pt_ref[bb * n_pages + cc * chunk_pages + j]
            out.append(pltpu.make_async_copy(
                ckv_hbm.at[pid], ckvbuf.at[slot, pl.ds(j * page, page), :], sem.at[slot, 0]))
            out.append(pltpu.make_async_copy(
                kpe_hbm.at[pid], kpebuf.at[slot, :, pl.ds(j * page, page)], sem.at[slot, 1]))
        return out

    @pl.when(n == 0)
    def _():
        for cp in copies(0, 0):
            cp.start()

    slot = n % 2

    @pl.when(n + 1 < total)
    def _():
        for cp in copies(n + 1, 1 - slot):
            cp.start()

    q = q_ref[0]
    q_lat = q[:, :KV_LORA]
    q_pe = q[:, KV_LORA:KV_LORA + MLA_ROPE_DIM]

    def scores(ckv, kpe_t):
        return (_dot_nt(q_lat, ckv) + _dot(q_pe, kpe_t)) * MLA_SCALE

    @pl.when(c == 0)
    def _():
        ckv = ckvnew_ref[0].astype(BF16)
        s = scores(ckv, kpenew_ref[0].astype(BF16))
        r_iota = lax.broadcasted_iota(jnp.int32, s.shape, 0)
        c_iota = lax.broadcasted_iota(jnp.int32, s.shape, 1)
        s = jnp.where(c_iota <= r_iota // MLA_HEADS, s, NEG_INF)
        m = jnp.max(s, axis=-1, keepdims=True)
        p = jnp.exp(s - m)
        m_ref[...] = m
        l_ref[...] = jnp.sum(p, axis=-1, keepdims=True)
        acc_ref[...] = _dot(p.astype(BF16), ckv)

    for cp in copies(n, slot):
        cp.wait()

    ckv = ckvbuf[slot].astype(BF16)
    s = scores(ckv, kpebuf[slot].astype(BF16))
    m_old = m_ref[...]
    m_new = jnp.maximum(m_old, jnp.max(s, axis=-1, keepdims=True))
    p = jnp.exp(s - m_new)
    alpha = jnp.exp(m_old - m_new)
    l_new = alpha * l_ref[...] + jnp.sum(p, axis=-1, keepdims=True)
    acc_new = alpha * acc_ref[...] + _dot(p.astype(BF16), ckv)
    m_ref[...] = m_new
    l_ref[...] = l_new
    acc_ref[...] = acc_new

    @pl.when(c == n_chunks - 1)
    def _():
        o_ref[0] = acc_new / l_new


def _mla_decode_call(pt_flat, qcs, ckv_new, kpe_new_t, cache_ckv, cache_kpe_t, n_pages):
    nbatch, rows, _ = qcs.shape
    page = cache_ckv.shape[1]
    chunk_pages = min(MLAD_PAGES, n_pages)
    assert n_pages % chunk_pages == 0
    chunk = chunk_pages * page
    any_spec = pl.BlockSpec(memory_space=pl.ANY)
    per_b = lambda shape: pl.BlockSpec((1,) + shape, lambda b, c, pt: (b, 0, 0))
    kernel = functools.partial(_mla_decode_kernel, n_pages=n_pages)
    return pl.pallas_call(
        kernel,
        grid_spec=pltpu.PrefetchScalarGridSpec(
            num_scalar_prefetch=1,
            grid=(nbatch, n_pages // chunk_pages),
            in_specs=[per_b((rows, QCAT)), per_b((page, KV_LORA)), per_b((MLA_ROPE_DIM, page)),
                      any_spec, any_spec],
            out_specs=per_b((rows, KV_LORA)),
            scratch_shapes=[
                pltpu.VMEM((2, chunk, KV_LORA), F32),
                pltpu.VMEM((2, MLA_ROPE_DIM, chunk), F32),
                pltpu.VMEM((rows, 1), F32),
                pltpu.VMEM((rows, 1), F32),
                pltpu.VMEM((rows, KV_LORA), F32),
                pltpu.SemaphoreType.DMA((2, 2)),
            ]),
        out_shape=jax.ShapeDtypeStruct((nbatch, rows, KV_LORA), F32),
        compiler_params=pltpu.CompilerParams(
            dimension_semantics=("arbitrary", "arbitrary"), vmem_limit_bytes=VMEM_LIMIT_BYTES),
        name="mla_decode",
    )(pt_flat, qcs, ckv_new, kpe_new_t, cache_ckv, cache_kpe_t)


def _prep_ffn(g_pre, w_gate, w_up, w_down, g_post):
    d, d_ff = w_gate.shape
    assert d_ff % FFN_CHUNK == 0
    return (g_pre.reshape(1, d), w_gate.astype(BF16), w_up.astype(BF16), w_down.astype(BF16),
            g_post.reshape(1, d))


def _prep_proj(g_mix_pre, w_in, g_q_lora, w_q_up, w_uk, g_kv_lora):
    d = w_in.shape[0]
    half = MLA_ROPE_DIM // 2
    kpe_w = w_in[:, _IN_KPE:_IN_KPE + MLA_ROPE_DIM]
    pad = jnp.zeros((d, LANES - MLA_ROPE_DIM), w_in.dtype)
    w_in_ext = jnp.concatenate([
        w_in[:, :SB_WIDTH] * SB_SCALE, w_in[:, SB_WIDTH:_IN_KPE],
        kpe_w, pad, kpe_w[:, half:], kpe_w[:, :half], pad], axis=1).astype(BF16)

    wq = w_q_up.reshape(Q_LORA, MLA_HEADS, MLA_QK_DIM)
    nope = wq[:, :, :MLA_NOPE_DIM].reshape(Q_LORA, MLA_HEADS * MLA_NOPE_DIM)
    pe = wq[:, :, MLA_NOPE_DIM:]
    pe_swap = jnp.concatenate([pe[:, :, half:], pe[:, :, :half]], axis=-1)
    lane_pad = lambda a: jnp.pad(a, ((0, 0), (0, 0), (0, LANES - MLA_ROPE_DIM))).reshape(Q_LORA, MLA_HEADS * LANES)
    w_qup_ext = jnp.concatenate([nope, lane_pad(pe), lane_pad(pe_swap)], axis=1).astype(BF16)

    eye = jnp.eye(MLA_HEADS, dtype=w_uk.dtype)
    w_uk_bd = jnp.einsum('chn,hg->hngc', w_uk, eye).reshape(MLA_HEADS * MLA_NOPE_DIM, MLA_HEADS * KV_LORA)
    return (g_mix_pre.reshape(1, d), w_in_ext, g_q_lora.reshape(1, Q_LORA), w_qup_ext,
            w_uk_bd.astype(BF16), g_kv_lora.reshape(1, KV_LORA))


def _prep_out(g_sb_out, g_mla_out, w_uv, w_out, g_mix_post):
    eye = jnp.eye(MLA_HEADS, dtype=w_uv.dtype)
    w_uv_bd = jnp.einsum('chv,hg->hcgv', w_uv, eye).reshape(MLA_HEADS * KV_LORA, MLA_WIDTH)
    return (g_sb_out.reshape(1, SB_WIDTH), g_mla_out.reshape(1, MLA_WIDTH), w_uv_bd.astype(BF16),
            w_out.astype(BF16), g_mix_post.reshape(1, -1))


def _rope_tables(pos):
    half = MLA_ROPE_DIM // 2
    inv_freq = ROPE_THETA ** (-jnp.arange(half, dtype=F32) / half)
    ang = pos.astype(F32)[:, None] * inv_freq[None, :]
    cos, sin = jnp.cos(ang), jnp.sin(ang)
    pad = jnp.zeros((pos.shape[0], LANES - MLA_ROPE_DIM), F32)
    return (jnp.concatenate([cos, cos, pad], axis=1), jnp.concatenate([-sin, sin, pad], axis=1))


def _token_tile(t):
    for tm in (256, 128):
        if t % tm == 0:
            return tm
    raise ValueError(f"token count {t} is not a multiple of 128")


def kernel(x_prompt, x_sample, cache_sb_k, cache_sb_v, cache_mla_ckv, cache_mla_kpe, page_table,
           g_ffn1_pre, w_ffn1_gate, w_ffn1_up, w_ffn1_down, g_ffn1_post,
           g_mix_pre, w_in, g_q_lora, w_q_up, g_kv_lora, w_uk, w_uv, g_sb_out, g_mla_out, w_out, g_mix_post,
           g_ffn2_pre, w_ffn2_gate, w_ffn2_up, w_ffn2_down, g_ffn2_post):
    batch, seq, d = x_prompt.shape
    dec_batch, dec_seq, _ = x_sample.shape
    page = cache_sb_k.shape[1]
    n_pages = page_table.shape[1]
    past_len = n_pages * page
    assert seq % MLA_CHUNK == 0 and MLA_CHUNK % ATTN_BLOCK == 0 and dec_seq <= SUBLANES
    assert n_pages % SBD_PAGES == 0 and n_pages >= 2 * SBD_PAGES
    assert dec_batch % SBD_GROUP == 0

    ffn1 = _prep_ffn(g_ffn1_pre, w_ffn1_gate, w_ffn1_up, w_ffn1_down, g_ffn1_post)
    ffn2 = _prep_ffn(g_ffn2_pre, w_ffn2_gate, w_ffn2_up, w_ffn2_down, g_ffn2_post)
    proj_w = _prep_proj(g_mix_pre, w_in, g_q_lora, w_q_up, w_uk, g_kv_lora)
    out_w = _prep_out(g_sb_out, g_mla_out, w_uv, w_out, g_mix_post)

    cos_p, sin_p = _rope_tables(jnp.arange(seq))
    tm_p = _token_tile(seq)
    (x1_p, sbk_p, sbv_p, ckv_p, kpe_p, qa_p, ka_p, va_p, qcat_p, kcat_p) = _ffn_proj_call(
        x_prompt, cos_p, sin_p, ffn1, proj_w, tm_p)
    osb_p, olat_p = _prompt_attn_call(qa_p, ka_p, va_p, qcat_p, kcat_p)
    y_p = _out_ffn_call(osb_p, olat_p, x1_p, out_w, ffn2, tm_p)

    n_tok = dec_batch * dec_seq
    cos_s, sin_s = _rope_tables(past_len + (jnp.arange(n_tok) % dec_seq))
    tm_s = _token_tile(n_tok)
    (x1_s, sbk_s, sbv_s, ckv_s, kpe_s, qa_s, _, _, qcat_s, _) = _ffn_proj_call(
        x_sample.reshape(1, n_tok, d), cos_s, sin_s, ffn1, proj_w, tm_s)

    pt_flat = page_table.reshape(-1)
    pad_tok = page - dec_seq
    q8 = qa_s[0].transpose(1, 0, 2).reshape(dec_batch, dec_seq, SB_HEADS, SB_HEAD_DIM).transpose(0, 2, 1, 3)
    q8 = jnp.pad(q8, ((0, 0), (0, 0), (0, SUBLANES - dec_seq), (0, 0))).astype(F32)
    head_dim_tok = lambda a: jnp.pad(
        a.reshape(dec_batch, dec_seq, SB_HEADS, SB_HEAD_DIM).transpose(0, 2, 3, 1),
        ((0, 0), (0, 0), (0, 0), (0, pad_tok)))
    o8 = _sb_decode_call(pt_flat, q8, head_dim_tok(sbk_s), head_dim_tok(sbv_s),
                         cache_sb_k.transpose(0, 2, 3, 1), cache_sb_v.transpose(0, 2, 3, 1), n_pages, dec_seq)
    osb_s = o8[:, :, :dec_seq, :].transpose(0, 2, 1, 3).reshape(n_tok, SB_PAIRS, LANES).transpose(1, 0, 2)[None]

    qcs = qcat_s[0].reshape(MLA_HEADS, dec_batch, dec_seq, QCAT).transpose(1, 2, 0, 3)
    qcs = qcs.reshape(dec_batch, dec_seq * MLA_HEADS, QCAT)
    ckv_new = jnp.pad(ckv_s.reshape(dec_batch, dec_seq, KV_LORA), ((0, 0), (0, pad_tok), (0, 0)))
    kpe_new_t = jnp.pad(kpe_s.reshape(dec_batch, dec_seq, MLA_ROPE_DIM).transpose(0, 2, 1),
                        ((0, 0), (0, 0), (0, pad_tok)))
    olat = _mla_decode_call(pt_flat, qcs, ckv_new, kpe_new_t, cache_mla_ckv,
                            cache_mla_kpe.transpose(0, 2, 1), n_pages)
    olat_s = olat.reshape(dec_batch, dec_seq, MLA_HEADS, KV_LORA).transpose(2, 0, 1, 3)
    olat_s = olat_s.reshape(1, MLA_HEADS, n_tok, KV_LORA)

    y_s = _out_ffn_call(osb_s, olat_s, x1_s, out_w, ffn2, tm_s)

    return (y_p, y_s.reshape(dec_batch, dec_seq, d),
            sbk_p.reshape(batch, seq, SB_HEADS, SB_HEAD_DIM), sbv_p.reshape(batch, seq, SB_HEADS, SB_HEAD_DIM),
            ckv_p, kpe_p,
            sbk_s.reshape(dec_batch, dec_seq, SB_HEADS, SB_HEAD_DIM),
            sbv_s.reshape(dec_batch, dec_seq, SB_HEADS, SB_HEAD_DIM),
            ckv_s.reshape(dec_batch, dec_seq, KV_LORA), kpe_s.reshape(dec_batch, dec_seq, MLA_ROPE_DIM))
```

```python
import functools

import jax
import jax.numpy as jnp
from jax import lax
from jax.experimental import pallas as pl
from jax.experimental.pallas import tpu as pltpu

F32 = jnp.float32
BF16 = jnp.bfloat16

SB_HEADS = 8
SB_HEAD_DIM = 64
SB_WIDTH = SB_HEADS * SB_HEAD_DIM
SB_PAIRS = SB_HEADS // 2
MLA_HEADS = 8
MLA_NOPE_DIM = 64
MLA_ROPE_DIM = 32
MLA_V_DIM = 64
MLA_QK_DIM = MLA_NOPE_DIM + MLA_ROPE_DIM
MLA_WIDTH = MLA_HEADS * MLA_V_DIM
Q_LORA = 256
KV_LORA = 128
FFN_RESIDUAL_WEIGHT = 0.5
ROPE_THETA = 10000.0
NORM_EPS = 1e-6
SB_SCALE = SB_HEAD_DIM ** -0.5
MLA_SCALE = MLA_QK_DIM ** -0.5
LOG2_E = 1.4426950408889634
MLA_EXP2_SCALE = MLA_SCALE * LOG2_E
NEG_INF = -1e30

LANES = 128
SUBLANES = 8
MXU_DIM = 256
VMEM_LIMIT_BYTES = 56 * 1024 * 1024

FFN_CHUNK = MXU_DIM
ATTN_BLOCK = MXU_DIM
MLA_CHUNK = 2 * MXU_DIM
MLA_GROUPS = 1
QCAT = 2 * LANES
SB_EXIT = 104.0

SBD_GROUP = 4
SBD_PAGES = 2
MLAD_PAGES = 64


def _dot(a, b):
    return jnp.dot(a, b, preferred_element_type=F32)


def _dot_nt(a, b):
    return lax.dot_general(a, b, (((1,), (1,)), ((), ())), preferred_element_type=F32)


def _rms(x, g):
    ms = jnp.mean(x * x, axis=-1, keepdims=True)
    return x * lax.rsqrt(ms + NORM_EPS) * g


def _softplus(z):
    return jnp.maximum(z, 0.0) + jnp.log(1.0 + jnp.exp(-jnp.abs(z)))


def _split_bf16(x):
    hi = x.astype(BF16)
    lo = (x - hi.astype(F32)).astype(BF16)
    return hi, lo


def _suffix_sums(x, tri):
    hi, lo = _split_bf16(x)
    return _dot(hi, tri) + _dot(lo, tri)


def _tri(n):
    row = lax.broadcasted_iota(jnp.int32, (n, n), 0)
    col = lax.broadcasted_iota(jnp.int32, (n, n), 1)
    return (row >= col).astype(BF16)


def _const_spec(shape):
    zeros = (0,) * len(shape)
    return pl.BlockSpec(shape, lambda *_: zeros, pipeline_mode=pl.Buffered(1))


def _half_step_ffn(x, ffn_refs, act_ref):
    g_pre_ref, wg_ref, wu_ref, wd_ref, g_post_ref = ffn_refs
    h = _rms(x, g_pre_ref[...]).astype(BF16)
    for c in range(wg_ref.shape[1] // FFN_CHUNK):
        cols = slice(c * FFN_CHUNK, (c + 1) * FFN_CHUNK)
        g = _dot(h, wg_ref[:, cols])
        u = _dot(h, wu_ref[:, cols])
        act_ref[:, cols] = ((g * jax.nn.sigmoid(g)) * u).astype(BF16)
    y = _dot(act_ref[...], wd_ref[...])
    return x + FFN_RESIDUAL_WEIGHT * _rms(y, g_post_ref[...])


_IN_Q = 0
_IN_K = SB_WIDTH
_IN_V = 2 * SB_WIDTH
_IN_QD = 3 * SB_WIDTH
_IN_KVD = _IN_QD + Q_LORA
_IN_KPE = _IN_KVD + KV_LORA
_IN_KPE_SWAP = _IN_KPE + LANES
_IN_COLS_EXT = _IN_KPE_SWAP + LANES
_QUP_NOPE = 0
_QUP_PE = MLA_HEADS * MLA_NOPE_DIM
_QUP_PE_SWAP = _QUP_PE + MLA_HEADS * LANES
_QUP_COLS_EXT = _QUP_PE_SWAP + MLA_HEADS * LANES


def _ffn_proj_kernel(x_ref, cos_ref, sin_ref, g_pre_ref, wg_ref, wu_ref, wd_ref, g_post_ref,
                     g_mix_ref, w_in_ref, g_q_ref, w_qup_ref, w_uk_ref, g_kv_ref,
                     x1_ref, sbk_ref, sbv_ref, ckv_ref, kpe_ref,
                     qa_ref, ka_ref, va_ref, qcat_ref, kcat_ref, ckvt_ref, act_ref):
    x1 = _half_step_ffn(x_ref[0], (g_pre_ref, wg_ref, wu_ref, wd_ref, g_post_ref), act_ref)
    x1_ref[0] = x1

    hm = _rms(x1, g_mix_ref[...]).astype(BF16)
    proj = _dot(hm, w_in_ref[...])
    sbk = proj[:, _IN_K:_IN_K + SB_WIDTH]
    sbv = proj[:, _IN_V:_IN_V + SB_WIDTH]
    sbk_ref[0] = sbk.T
    sbv_ref[0] = sbv.T
    for p in range(SB_PAIRS):
        lo, hi = p * LANES, (p + 1) * LANES
        qa_ref[0, p] = proj[:, _IN_Q + lo:_IN_Q + hi].astype(BF16)
        ka_ref[0, p] = sbk[:, lo:hi].astype(BF16)
        va_ref[0, p] = sbv[:, lo:hi].astype(BF16)

    cos = cos_ref[...]
    sin = sin_ref[...]
    qn = _rms(proj[:, _IN_QD:_IN_QD + Q_LORA], g_q_ref[...]).astype(BF16)
    qup = _dot(qn, w_qup_ref[...])
    qlat = _dot(qup[:, _QUP_NOPE:_QUP_PE].astype(BF16), w_uk_ref[...])
    for h in range(MLA_HEADS):
        lo, hi = h * LANES, (h + 1) * LANES
        pe = (qup[:, _QUP_PE + lo:_QUP_PE + hi] * cos
              + qup[:, _QUP_PE_SWAP + lo:_QUP_PE_SWAP + hi] * sin)
        qcat_ref[0, h, :, :LANES] = qlat[:, lo:hi].astype(BF16)
        qcat_ref[0, h, :, LANES:] = pe.astype(BF16)

    ckv = _rms(proj[:, _IN_KVD:_IN_KVD + KV_LORA], g_kv_ref[...])
    kpe = (proj[:, _IN_KPE:_IN_KPE + LANES] * cos
           + proj[:, _IN_KPE_SWAP:_IN_KPE_SWAP + LANES] * sin)
    ckv_ref[0] = ckv
    kpe_ref[0] = kpe[:, :MLA_ROPE_DIM]
    kcat_ref[0, :, :LANES] = ckv.astype(BF16)
    kcat_ref[0, :, LANES:] = kpe.astype(BF16)
    ckvt_ref[0] = ckv.T.astype(BF16)


def _ffn_proj_call(x, cos_t, sin_t, ffn_w, proj_w, tm):
    nb, t, d = x.shape
    weights = tuple(ffn_w) + tuple(proj_w)
    d_ff = ffn_w[1].shape[1]
    tok = lambda shape: pl.BlockSpec((1, tm) + shape, lambda b, i: (b, i) + (0,) * len(shape))
    heads = lambda n, w: pl.BlockSpec((1, n, tm, w), lambda b, i: (b, 0, i, 0))
    tab = pl.BlockSpec((tm, LANES), lambda b, i: (i, 0))
    in_specs = [tok((d,)), tab, tab] + [_const_spec(w.shape) for w in weights]
    out_shape = (
        jax.ShapeDtypeStruct((nb, t, d), F32),
        jax.ShapeDtypeStruct((nb, SB_WIDTH, t), F32),
        jax.ShapeDtypeStruct((nb, SB_WIDTH, t), F32),
        jax.ShapeDtypeStruct((nb, t, KV_LORA), F32),
        jax.ShapeDtypeStruct((nb, t, MLA_ROPE_DIM), F32),
        jax.ShapeDtypeStruct((nb, SB_PAIRS, t, LANES), BF16),
        jax.ShapeDtypeStruct((nb, SB_PAIRS, t, LANES), BF16),
        jax.ShapeDtypeStruct((nb, SB_PAIRS, t, LANES), BF16),
        jax.ShapeDtypeStruct((nb, MLA_HEADS, t, QCAT), BF16),
        jax.ShapeDtypeStruct((nb, t, QCAT), BF16),
        jax.ShapeDtypeStruct((nb, KV_LORA, t), BF16),
    )
    tok_minor = lambda n: pl.BlockSpec((1, n, tm), lambda b, i: (b, 0, i))
    out_specs = (tok((d,)), tok_minor(SB_WIDTH), tok_minor(SB_WIDTH), tok((KV_LORA,)), tok((MLA_ROPE_DIM,)),
                 heads(SB_PAIRS, LANES), heads(SB_PAIRS, LANES), heads(SB_PAIRS, LANES),
                 heads(MLA_HEADS, QCAT), tok((QCAT,)), tok_minor(KV_LORA))
    return pl.pallas_call(
        _ffn_proj_kernel,
        grid=(nb, t // tm),
        in_specs=in_specs,
        out_specs=out_specs,
        out_shape=out_shape,
        scratch_shapes=[pltpu.VMEM((tm, d_ff), BF16)],
        compiler_params=pltpu.CompilerParams(
            dimension_semantics=("parallel", "parallel"), vmem_limit_bytes=VMEM_LIMIT_BYTES),
        name="ffn_proj",
    )(x, cos_t, sin_t, *weights)


def _out_ffn_kernel(osb_ref, olat_ref, x1_ref, g_sb_ref, g_mla_ref, w_uv_ref, w_out_ref, g_mixpost_ref,
                    g_pre_ref, wg_ref, wu_ref, wd_ref, g_post_ref, y_ref, act_ref):
    osb = jnp.concatenate([osb_ref[0, p] for p in range(SB_PAIRS)], axis=-1)
    olat = jnp.concatenate([olat_ref[0, h] for h in range(MLA_HEADS)], axis=-1)
    omla = _dot(olat.astype(BF16), w_uv_ref[...])
    merged = jnp.concatenate([_rms(osb, g_sb_ref[...]), _rms(omla, g_mla_ref[...])], axis=-1)
    o = _dot(merged.astype(BF16), w_out_ref[...])
    x2 = x1_ref[0] + _rms(o, g_mixpost_ref[...])
    y_ref[0] = _half_step_ffn(x2, (g_pre_ref, wg_ref, wu_ref, wd_ref, g_post_ref), act_ref)


def _out_ffn_call(osb, olat, x1, out_w, ffn_w, tm):
    nb, t, d = x1.shape
    d_ff = ffn_w[1].shape[1]
    heads = lambda n: pl.BlockSpec((1, n, tm, LANES), lambda b, i: (b, 0, i, 0))
    tok = pl.BlockSpec((1, tm, d), lambda b, i: (b, i, 0))
    weights = tuple(out_w) + tuple(ffn_w)
    return pl.pallas_call(
        _out_ffn_kernel,
        grid=(nb, t // tm),
        in_specs=[heads(SB_PAIRS), heads(MLA_HEADS), tok] + [_const_spec(w.shape) for w in weights],
        out_specs=tok,
        out_shape=jax.ShapeDtypeStruct((nb, t, d), F32),
        scratch_shapes=[pltpu.VMEM((tm, d_ff), BF16)],
        compiler_params=pltpu.CompilerParams(
            dimension_semantics=("parallel", "parallel"), vmem_limit_bytes=VMEM_LIMIT_BYTES),
        name="out_ffn",
    )(osb, olat, x1, *weights)


def _prompt_attn_kernel(qa_ref, ka_ref, va_ref, qcat_ref, kcat_ref, ckvt_ref, osb_ref, olat_ref):
    qi = pl.program_id(1)
    blk = ATTN_BLOCK
    tri = _tri(blk)
    rows_sb = lax.broadcasted_iota(jnp.int32, (SB_HEADS * blk, blk), 0)
    strict = lax.broadcasted_iota(jnp.int32, (SB_HEADS * blk, blk), 1) < rows_sb % blk
    even_q = jnp.logical_xor(lax.broadcasted_iota(jnp.int32, (2 * blk, LANES), 1) >= SB_HEAD_DIM,
                             lax.broadcasted_iota(jnp.int32, (2 * blk, LANES), 0) < blk)
    diag = pl.multiple_of(qi * blk, blk)

    def pair_operands(p, off):
        k = ka_ref[0, p, pl.ds(off, blk), :]
        v = va_ref[0, p, pl.ds(off, blk), :]
        v2 = jnp.concatenate([v, v], axis=0)
        return k, jnp.where(even_q, v2, jnp.zeros_like(v2))

    q2s = []
    for p in range(SB_PAIRS):
        qp = qa_ref[0, p]
        qp2 = jnp.concatenate([qp, qp], axis=0)
        q2s.append(jnp.where(even_q, qp2, jnp.zeros_like(qp2)))

    def sb_step(off, carry, accs, masked):
        operands = [pair_operands(p, off) for p in range(SB_PAIRS)]
        z = jnp.concatenate([_dot_nt(q2s[p], operands[p][0]) for p in range(SB_PAIRS)], axis=0)
        sp = _softplus(z)
        if masked:
            sp = jnp.where(strict, sp, 0.0)
        suf = _suffix_sums(sp, tri)
        a = jnp.exp(z - suf - carry)
        if masked:
            a = jnp.where(strict, a, 0.0)
        a = a.astype(BF16)
        out = []
        for p in range(SB_PAIRS):
            a_even = a[2 * p * blk:(2 * p + 1) * blk]
            a_odd = a[(2 * p + 1) * blk:(2 * p + 2) * blk]
            out.append(accs[p] + _dot(jnp.concatenate([a_even, a_odd], axis=1), operands[p][1]))
        return carry + suf[:, 0:1], tuple(out)

    a0 = tuple(jnp.zeros((blk, LANES), F32) for _ in range(SB_PAIRS))
    carry, accs = sb_step(diag, jnp.zeros((SB_HEADS * blk, 1), F32), a0, True)

    def sb_cond(st):
        j, carry, _ = st
        return jnp.logical_and(j >= 0, jnp.min(carry) < SB_EXIT)

    def sb_body(st):
        j, carry, accs = st
        carry, accs = sb_step(pl.multiple_of(j * blk, blk), carry, accs, False)
        return j - 1, carry, accs

    _, _, accs = lax.while_loop(sb_cond, sb_body, (qi - 1, carry, accs))
    for p in range(SB_PAIRS):
        osb_ref[0, p] = accs[p]

    ch = MLA_CHUNK

    gh = MLA_HEADS // MLA_GROUPS
    cols = gh * blk
    qcs = [qcat_ref[0, g * gh:(g + 1) * gh].reshape(cols, QCAT) for g in range(MLA_GROUPS)]

    def mla_chunk(off, state, masked):
        kc = kcat_ref[0, pl.ds(off, ch), :]
        ckv_t = ckvt_ref[0, :, pl.ds(off, ch)]
        if masked:
            k_pos = off + lax.broadcasted_iota(jnp.int32, (ch, cols), 0)
            visible = k_pos <= diag + lax.broadcasted_iota(jnp.int32, (ch, cols), 1) % blk
        out = []
        for g in range(MLA_GROUPS):
            m, l, acc = state[g]
            s = _dot_nt(kc, qcs[g])
            if masked:
                s = jnp.where(visible, s, NEG_INF)
            m_new = jnp.maximum(m, jnp.max(s, axis=0, keepdims=True))
            p = jnp.exp2((s - m_new) * MLA_EXP2_SCALE)
            alpha = jnp.exp2((m - m_new) * MLA_EXP2_SCALE)
            l = alpha * l + jnp.sum(p, axis=0, keepdims=True)
            acc = alpha * acc + _dot(ckv_t, p.astype(BF16))
            out.append((m_new, l, acc))
        return tuple(out)

    n_full = diag // ch
    group0 = (jnp.full((1, cols), NEG_INF, F32), jnp.zeros((1, cols), F32), jnp.zeros((KV_LORA, cols), F32))
    state = lax.fori_loop(0, n_full, lambda j, st: mla_chunk(pl.multiple_of(j * ch, ch), st, False),
                          (group0,) * MLA_GROUPS)
    state = mla_chunk(pl.multiple_of(n_full * ch, ch), state, True)
    for g in range(MLA_GROUPS):
        _, l, acc = state[g]
        o_t = acc / l
        for i in range(gh):
            olat_ref[0, g * gh + i] = o_t[:, i * blk:(i + 1) * blk].T


def _prompt_attn_call(qa, ka, va, qcat, kcat, ckvt):
    nb, _, t, _ = qa.shape
    blk = ATTN_BLOCK
    q_spec = lambda n, w: pl.BlockSpec((1, n, blk, w), lambda b, i: (b, 0, i, 0))
    kv_spec = pl.BlockSpec((1, SB_PAIRS, t, LANES), lambda b, i: (b, 0, 0, 0))
    return pl.pallas_call(
        _prompt_attn_kernel,
        grid=(nb, t // blk),
        in_specs=[q_spec(SB_PAIRS, LANES), kv_spec, kv_spec, q_spec(MLA_HEADS, QCAT),
                  pl.BlockSpec((1, t, QCAT), lambda b, i: (b, 0, 0)),
                  pl.BlockSpec((1, KV_LORA, t), lambda b, i: (b, 0, 0))],
        out_specs=(q_spec(SB_PAIRS, LANES), q_spec(MLA_HEADS, KV_LORA)),
        out_shape=(jax.ShapeDtypeStruct((nb, SB_PAIRS, t, LANES), F32),
                   jax.ShapeDtypeStruct((nb, MLA_HEADS, t, KV_LORA), F32)),
        compiler_params=pltpu.CompilerParams(
            dimension_semantics=("parallel", "parallel"), vmem_limit_bytes=VMEM_LIMIT_BYTES),
        name="prompt_attn",
    )(qa, ka, va, qcat, kcat, ckvt)


def _sb_decode_kernel(pt_ref, q_ref, knew_hbm, vnew_hbm, kc_hbm, vc_hbm, o_ref,
                      kbuf, vbuf, kslow, vslow, sem, slow_sem, *, n_pages, n_new):
    step = pl.program_id(0)
    n_steps = pl.num_programs(0)
    grp = q_ref.shape[0]
    page = kc_hbm.shape[-1]
    first = SBD_PAGES * page
    ntok = first + page

    def first_copies(s, slot):
        out = []
        for g in range(grp):
            b = s * grp + g
            for hbm, new, buf, t in ((kc_hbm, knew_hbm, kbuf, 0), (vc_hbm, vnew_hbm, vbuf, 1)):
                for j in range(SBD_PAGES):
                    pid = pt_ref[b * n_pages + (n_pages - SBD_PAGES + j)]
                    out.append(pltpu.make_async_copy(
                        hbm.at[pid], buf.at[slot * grp + g, :, :, pl.ds(j * page, page)], sem.at[slot, t]))
                tile = pl.multiple_of((b * n_new) // page * page, page)
                out.append(pltpu.make_async_copy(
                    new.at[:, :, pl.ds(tile, page)], buf.at[slot * grp + g, :, :, pl.ds(first, page)],
                    sem.at[slot, t]))
        return out

    @pl.when(step == 0)
    def _():
        for c in first_copies(0, 0):
            c.start()

    slot = step % 2

    @pl.when(step + 1 < n_steps)
    def _():
        for c in first_copies(step + 1, 1 - slot):
            c.start()

    for c in first_copies(step, slot):
        c.wait()

    rows = SB_HEADS * SUBLANES
    r_iota = lax.broadcasted_iota(jnp.int32, (rows, ntok), 0)
    c_iota = lax.broadcasted_iota(jnp.int32, (rows, ntok), 1)
    real_row = (lax.broadcasted_iota(jnp.int32, (rows, 1), 0) % SUBLANES) < n_new
    tri = _tri(ntok)
    tri_slow = _tri(first)

    def scores(qs, buf, idx):
        return jnp.concatenate([_dot(qs[h], buf[idx, h].astype(BF16)) for h in range(SB_HEADS)], axis=0)

    def weighted(a, buf, idx):
        return jnp.concatenate(
            [_dot_nt(a[h * SUBLANES:(h + 1) * SUBLANES], buf[idx, h].astype(BF16)) for h in range(SB_HEADS)],
            axis=0)

    def newest_block(g):
        idx = slot * grp + g
        j_new = c_iota - (first + ((step * grp + g) * n_new) % page)
        valid = jnp.logical_or(c_iota < first, jnp.logical_and(j_new >= 0, j_new < r_iota % SUBLANES))
        qs = [q_ref[g, h].astype(BF16) for h in range(SB_HEADS)]
        z = scores(qs, kbuf, idx)
        sp = jnp.where(valid, _softplus(z), 0.0)
        suf = _suffix_sums(sp, tri)
        a = jnp.where(valid, jnp.exp(z - suf), 0.0).astype(BF16)
        return qs, suf[:, 0:1], weighted(a, vbuf, idx)

    newest = [newest_block(g) for g in range(grp)]

    for g in range(grp):
        b = step * grp + g
        qs, carry, acc = newest[g]

        def cond(st):
            c, carry, _ = st
            live = jnp.min(jnp.where(real_row, carry, SB_EXIT)) < SB_EXIT
            return jnp.logical_and(c >= 0, live)

        def body(st):
            c, carry, acc = st
            copies = []
            for hbm, buf, t in ((kc_hbm, kslow, 0), (vc_hbm, vslow, 1)):
                for j in range(SBD_PAGES):
                    pid = pt_ref[b * n_pages + c * SBD_PAGES + j]
                    copies.append(pltpu.make_async_copy(
                        hbm.at[pid], buf.at[0, :, :, pl.ds(j * page, page)], slow_sem.at[t]))
            for cp in copies:
                cp.start()
            for cp in copies:
                cp.wait()
            z = scores(qs, kslow, 0)
            suf = _suffix_sums(_softplus(z), tri_slow)
            a = jnp.exp(z - suf - carry).astype(BF16)
            return c - 1, carry + suf[:, 0:1], acc + weighted(a, vslow, 0)

        _, _, acc = lax.while_loop(cond, body, (n_pages // SBD_PAGES - 2, carry, acc))
        o_ref[g] = acc.reshape(SB_HEADS, SUBLANES, SB_HEAD_DIM)


def _sb_decode_call(pt_flat, q8, knew_t, vnew_t, cache_kt, cache_vt, n_pages, n_new):
    nbatch = q8.shape[0]
    page = cache_kt.shape[-1]
    grp = SBD_GROUP
    ntok = (SBD_PAGES + 1) * page
    any_spec = pl.BlockSpec(memory_space=pl.ANY)
    blk = pl.BlockSpec((grp, SB_HEADS, SUBLANES, SB_HEAD_DIM), lambda s, pt: (s, 0, 0, 0))
    kernel = functools.partial(_sb_decode_kernel, n_pages=n_pages, n_new=n_new)
    return pl.pallas_call(
        kernel,
        grid_spec=pltpu.PrefetchScalarGridSpec(
            num_scalar_prefetch=1,
            grid=(nbatch // grp,),
            in_specs=[blk, any_spec, any_spec, any_spec, any_spec],
            out_specs=blk,
            scratch_shapes=[
                pltpu.VMEM((2 * grp, SB_HEADS, SB_HEAD_DIM, ntok), F32),
                pltpu.VMEM((2 * grp, SB_HEADS, SB_HEAD_DIM, ntok), F32),
                pltpu.VMEM((1, SB_HEADS, SB_HEAD_DIM, SBD_PAGES * page), F32),
                pltpu.VMEM((1, SB_HEADS, SB_HEAD_DIM, SBD_PAGES * page), F32),
                pltpu.SemaphoreType.DMA((2, 2)),
                pltpu.SemaphoreType.DMA((2,)),
            ]),
        out_shape=jax.ShapeDtypeStruct((nbatch, SB_HEADS, SUBLANES, SB_HEAD_DIM), F32),
        compiler_params=pltpu.CompilerParams(
            dimension_semantics=("arbitrary",), vmem_limit_bytes=VMEM_LIMIT_BYTES),
        name="sb_decode",
    )(pt_flat, q8, knew_t, vnew_t, cache_kt, cache_vt)


def _mla_decode_kernel(pt_ref, q_ref, ckvnew_ref, kpenew_ref, ckv_hbm, kpe_hbm, o_ref,
                       ckvbuf, kpebuf, m_ref, l_ref, acc_ref, sem, *, n_pages):
    b = pl.program_id(0)
    c = pl.program_id(1)
    n_chunks = pl.num_programs(1)
    n = b * n_chunks + c
    total = pl.num_programs(0) * n_chunks
    page = ckv_hbm.shape[1]
    chunk_pages = ckvbuf.shape[1] // page

    def copies(i, slot):
        bb = i // n_chunks
        cc = i % n_chunks
        out = []
        for j in range(chunk_pages):
            pid = pt_ref[bb * n_pages + cc * chunk_pages + j]
            out.append(pltpu.make_async_copy(
                ckv_hbm.at[pid], ckvbuf.at[slot, pl.ds(j * page, page), :], sem.at[slot, 0]))
            out.append(pltpu.make_async_copy(
                kpe_hbm.at[pid], kpebuf.at[slot, :, pl.ds(j * page, page)], sem.at[slot, 1]))
        return out

    @pl.when(n == 0)
    def _():
        for cp in copies(0, 0):
            cp.start()

    slot = n % 2

    @pl.when(n + 1 < total)
    def _():
        for cp in copies(n + 1, 1 - slot):
            cp.start()

    q = q_ref[0]
    q_lat = q[:, :KV_LORA]
    q_pe = q[:, KV_LORA:KV_LORA + MLA_ROPE_DIM]

    def scores(ckv, kpe_t):
        return (_dot_nt(q_lat, ckv) + _dot(q_pe, kpe_t)) * MLA_SCALE

    @pl.when(c == 0)
    def _():
        ckv = ckvnew_ref[0].astype(BF16)
        s = scores(ckv, kpenew_ref[0].astype(BF16))
        r_iota = lax.broadcasted_iota(jnp.int32, s.shape, 0)
        c_iota = lax.broadcasted_iota(jnp.int32, s.shape, 1)
        s = jnp.where(c_iota <= r_iota // MLA_HEADS, s, NEG_INF)
        m = jnp.max(s, axis=-1, keepdims=True)
        p = jnp.exp(s - m)
        m_ref[...] = m
        l_ref[...] = jnp.sum(p, axis=-1, keepdims=True)
        acc_ref[...] = _dot(p.astype(BF16), ckv)

    for cp in copies(n, slot):
        cp.wait()

    ckv = ckvbuf[slot].astype(BF16)
    s = scores(ckv, kpebuf[slot].astype(BF16))
    m_old = m_ref[...]
    m_new = jnp.maximum(m_old, jnp.max(s, axis=-1, keepdims=True))
    p = jnp.exp(s - m_new)
    alpha = jnp.exp(m_old - m_new)
    l_new = alpha * l_ref[...] + jnp.sum(p, axis=-1, keepdims=True)
    acc_new = alpha * acc_ref[...] + _dot(p.astype(BF16), ckv)
    m_ref[...] = m_new
    l_ref[...] = l_new
    acc_ref[...] = acc_new

    @pl.when(c == n_chunks - 1)
    def _():
        o_ref[0] = acc_new / l_new


def _mla_decode_call(pt_flat, qcs, ckv_new, kpe_new_t, cache_ckv, cache_kpe_t, n_pages):
    nbatch, rows, _ = qcs.shape
    page = cache_ckv.shape[1]
    chunk_pages = min(MLAD_PAGES, n_pages)
    assert n_pages % chunk_pages == 0
    chunk = chunk_pages * page
    any_spec = pl.BlockSpec(memory_space=pl.ANY)
    per_b = lambda shape: pl.BlockSpec((1,) + shape, lambda b, c, pt: (b, 0, 0))
    kernel = functools.partial(_mla_decode_kernel, n_pages=n_pages)
    return pl.pallas_call(
        kernel,
        grid_spec=pltpu.PrefetchScalarGridSpec(
            num_scalar_prefetch=1,
            grid=(nbatch, n_pages // chunk_pages),
            in_specs=[per_b((rows, QCAT)), per_b((page, KV_LORA)), per_b((MLA_ROPE_DIM, page)),
                      any_spec, any_spec],
            out_specs=per_b((rows, KV_LORA)),
            scratch_shapes=[
                pltpu.VMEM((2, chunk, KV_LORA), F32),
                pltpu.VMEM((2, MLA_ROPE_DIM, chunk), F32),
                pltpu.VMEM((rows, 1), F32),
                pltpu.VMEM((rows, 1), F32),
                pltpu.VMEM((rows, KV_LORA), F32),
                pltpu.SemaphoreType.DMA((2, 2)),
            ]),
        out_shape=jax.ShapeDtypeStruct((nbatch, rows, KV_LORA), F32),
        compiler_params=pltpu.CompilerParams(
            dimension_semantics=("arbitrary", "arbitrary"), vmem_limit_bytes=VMEM_LIMIT_BYTES),
        name="mla_decode",
    )(pt_flat, qcs, ckv_new, kpe_new_t, cache_ckv, cache_kpe_t)


def _prep_ffn(g_pre, w_gate, w_up, w_down, g_post):
    d, d_ff = w_gate.shape
    assert d_ff % FFN_CHUNK == 0
    return (g_pre.reshape(1, d), w_gate.astype(BF16), w_up.astype(BF16), w_down.astype(BF16),
            g_post.reshape(1, d))


def _prep_proj(g_mix_pre, w_in, g_q_lora, w_q_up, w_uk, g_kv_lora):
    d = w_in.shape[0]
    half = MLA_ROPE_DIM // 2
    kpe_w = w_in[:, _IN_KPE:_IN_KPE + MLA_ROPE_DIM]
    pad = jnp.zeros((d, LANES - MLA_ROPE_DIM), w_in.dtype)
    w_in_ext = jnp.concatenate([
        w_in[:, :SB_WIDTH] * SB_SCALE, w_in[:, SB_WIDTH:_IN_KPE],
        kpe_w, pad, kpe_w[:, half:], kpe_w[:, :half], pad], axis=1).astype(BF16)

    wq = w_q_up.reshape(Q_LORA, MLA_HEADS, MLA_QK_DIM)
    nope = wq[:, :, :MLA_NOPE_DIM].reshape(Q_LORA, MLA_HEADS * MLA_NOPE_DIM)
    pe = wq[:, :, MLA_NOPE_DIM:]
    pe_swap = jnp.concatenate([pe[:, :, half:], pe[:, :, :half]], axis=-1)
    lane_pad = lambda a: jnp.pad(a, ((0, 0), (0, 0), (0, LANES - MLA_ROPE_DIM))).reshape(Q_LORA, MLA_HEADS * LANES)
    w_qup_ext = jnp.concatenate([nope, lane_pad(pe), lane_pad(pe_swap)], axis=1).astype(BF16)

    eye = jnp.eye(MLA_HEADS, dtype=w_uk.dtype)
    w_uk_bd = jnp.einsum('chn,hg->hngc', w_uk, eye).reshape(MLA_HEADS * MLA_NOPE_DIM, MLA_HEADS * KV_LORA)
    return (g_mix_pre.reshape(1, d), w_in_ext, g_q_lora.reshape(1, Q_LORA), w_qup_ext,
            w_uk_bd.astype(BF16), g_kv_lora.reshape(1, KV_LORA))


def _prep_out(g_sb_out, g_mla_out, w_uv, w_out, g_mix_post):
    eye = jnp.eye(MLA_HEADS, dtype=w_uv.dtype)
    w_uv_bd = jnp.einsum('chv,hg->hcgv', w_uv, eye).reshape(MLA_HEADS * KV_LORA, MLA_WIDTH)
    return (g_sb_out.reshape(1, SB_WIDTH), g_mla_out.reshape(1, MLA_WIDTH), w_uv_bd.astype(BF16),
            w_out.astype(BF16), g_mix_post.reshape(1, -1))


def _rope_tables(pos):
    half = MLA_ROPE_DIM // 2
    inv_freq = ROPE_THETA ** (-jnp.arange(half, dtype=F32) / half)
    ang = pos.astype(F32)[:, None] * inv_freq[None, :]
    cos, sin = jnp.cos(ang), jnp.sin(ang)
    pad = jnp.zeros((pos.shape[0], LANES - MLA_ROPE_DIM), F32)
    return (jnp.concatenate([cos, cos, pad], axis=1), jnp.concatenate([-sin, sin, pad], axis=1))


def _token_tile(t):
    for tm in (256, 128):
        if t % tm == 0:
            return tm
    raise ValueError(f"token count {t} is not a multiple of 128")


def kernel(x_prompt, x_sample, cache_sb_k, cache_sb_v, cache_mla_ckv, cache_mla_kpe, page_table,
           g_ffn1_pre, w_ffn1_gate, w_ffn1_up, w_ffn1_down, g_ffn1_post,
           g_mix_pre, w_in, g_q_lora, w_q_up, g_kv_lora, w_uk, w_uv, g_sb_out, g_mla_out, w_out, g_mix_post,
           g_ffn2_pre, w_ffn2_gate, w_ffn2_up, w_ffn2_down, g_ffn2_post):
    batch, seq, d = x_prompt.shape
    dec_batch, dec_seq, _ = x_sample.shape
    page = cache_sb_k.shape[1]
    n_pages = page_table.shape[1]
    past_len = n_pages * page
    assert seq % MLA_CHUNK == 0 and MLA_CHUNK % ATTN_BLOCK == 0 and dec_seq <= SUBLANES
    assert n_pages % SBD_PAGES == 0 and n_pages >= 2 * SBD_PAGES
    assert dec_batch % SBD_GROUP == 0 and page % dec_seq == 0 and (dec_batch * dec_seq) % page == 0

    ffn1 = _prep_ffn(g_ffn1_pre, w_ffn1_gate, w_ffn1_up, w_ffn1_down, g_ffn1_post)
    ffn2 = _prep_ffn(g_ffn2_pre, w_ffn2_gate, w_ffn2_up, w_ffn2_down, g_ffn2_post)
    proj_w = _prep_proj(g_mix_pre, w_in, g_q_lora, w_q_up, w_uk, g_kv_lora)
    out_w = _prep_out(g_sb_out, g_mla_out, w_uv, w_out, g_mix_post)

    cos_p, sin_p = _rope_tables(jnp.arange(seq))
    tm_p = _token_tile(seq)
    (x1_p, sbk_p, sbv_p, ckv_p, kpe_p, qa_p, ka_p, va_p, qcat_p, kcat_p, ckvt_p) = _ffn_proj_call(
        x_prompt, cos_p, sin_p, ffn1, proj_w, tm_p)
    osb_p, olat_p = _prompt_attn_call(qa_p, ka_p, va_p, qcat_p, kcat_p, ckvt_p)
    y_p = _out_ffn_call(osb_p, olat_p, x1_p, out_w, ffn2, tm_p)

    n_tok = dec_batch * dec_seq
    cos_s, sin_s = _rope_tables(past_len + (jnp.arange(n_tok) % dec_seq))
    tm_s = _token_tile(n_tok)
    (x1_s, sbk_s, sbv_s, ckv_s, kpe_s, qa_s, _, _, qcat_s, _, _) = _ffn_proj_call(
        x_sample.reshape(1, n_tok, d), cos_s, sin_s, ffn1, proj_w, tm_s)

    pt_flat = page_table.reshape(-1)
    pad_tok = page - dec_seq
    q8 = qa_s[0].transpose(1, 0, 2).reshape(dec_batch, dec_seq, SB_HEADS, SB_HEAD_DIM).transpose(0, 2, 1, 3)
    q8 = jnp.pad(q8, ((0, 0), (0, 0), (0, SUBLANES - dec_seq), (0, 0))).astype(F32)
    head_dim_tok = lambda a: a.reshape(SB_HEADS, SB_HEAD_DIM, n_tok)
    o8 = _sb_decode_call(pt_flat, q8, head_dim_tok(sbk_s), head_dim_tok(sbv_s),
                         cache_sb_k.transpose(0, 2, 3, 1), cache_sb_v.transpose(0, 2, 3, 1), n_pages, dec_seq)
    osb_s = o8[:, :, :dec_seq, :].transpose(0, 2, 1, 3).reshape(n_tok, SB_PAIRS, LANES).transpose(1, 0, 2)[None]

    qcs = qcat_s[0].reshape(MLA_HEADS, dec_batch, dec_seq, QCAT).transpose(1, 2, 0, 3)
    qcs = qcs.reshape(dec_batch, dec_seq * MLA_HEADS, QCAT)
    ckv_new = jnp.pad(ckv_s.reshape(dec_batch, dec_seq, KV_LORA), ((0, 0), (0, pad_tok), (0, 0)))
    kpe_new_t = jnp.pad(kpe_s.reshape(dec_batch, dec_seq, MLA_ROPE_DIM).transpose(0, 2, 1),
                        ((0, 0), (0, 0), (0, pad_tok)))
    olat = _mla_decode_call(pt_flat, qcs, ckv_new, kpe_new_t, cache_mla_ckv,
                            cache_mla_kpe.transpose(0, 2, 1), n_pages)
    olat_s = olat.reshape(dec_batch, dec_seq, MLA_HEADS, KV_LORA).transpose(2, 0, 1, 3)
    olat_s = olat_s.reshape(1, MLA_HEADS, n_tok, KV_LORA)

    y_s = _out_ffn_call(osb_s, olat_s, x1_s, out_w, ffn2, tm_s)

    tok_head_dim = lambda a, lead: a.reshape(lead + (SB_HEADS, SB_HEAD_DIM, -1)).transpose(
        tuple(range(len(lead))) + (len(lead) + 2, len(lead), len(lead) + 1))
    return (y_p, y_s.reshape(dec_batch, dec_seq, d),
            tok_head_dim(sbk_p, (batch,)), tok_head_dim(sbv_p, (batch,)),
            ckv_p, kpe_p,
            tok_head_dim(sbk_s[0], ()).reshape(dec_batch, dec_seq, SB_HEADS, SB_HEAD_DIM),
            tok_head_dim(sbv_s[0], ()).reshape(dec_batch, dec_seq, SB_HEADS, SB_HEAD_DIM),
            ckv_s.reshape(dec_batch, dec_seq, KV_LORA), kpe_s.reshape(dec_batch, dec_seq, MLA_ROPE_DIM))
```

```python
import functools

import jax
import jax.numpy as jnp
from jax import lax
from jax.experimental import pallas as pl
from jax.experimental.pallas import tpu as pltpu

F32 = jnp.float32
BF16 = jnp.bfloat16

SB_HEADS = 8
SB_HEAD_DIM = 64
SB_WIDTH = SB_HEADS * SB_HEAD_DIM
SB_PAIRS = SB_HEADS // 2
MLA_HEADS = 8
MLA_NOPE_DIM = 64
MLA_ROPE_DIM = 32
MLA_V_DIM = 64
MLA_QK_DIM = MLA_NOPE_DIM + MLA_ROPE_DIM
MLA_WIDTH = MLA_HEADS * MLA_V_DIM
Q_LORA = 256
KV_LORA = 128
FFN_RESIDUAL_WEIGHT = 0.5
ROPE_THETA = 10000.0
NORM_EPS = 1e-6
SB_SCALE = SB_HEAD_DIM ** -0.5
MLA_SCALE = MLA_QK_DIM ** -0.5
LOG2_E = 1.4426950408889634
MLA_EXP2_SCALE = MLA_SCALE * LOG2_E
NEG_INF = -1e30

LANES = 128
SUBLANES = 8
MXU_DIM = 256
VMEM_LIMIT_BYTES = 56 * 1024 * 1024

FFN_CHUNK = MXU_DIM
ATTN_BLOCK = MXU_DIM
MLA_CHUNK = 2 * MXU_DIM
CKVT_ROWS = KV_LORA + 16
QCAT = 2 * LANES
SB_EXIT = 104.0

SBD_GROUP = 4
SBD_PAGES = 2
MLAD_PAGES = 64


def _dot(a, b):
    return jnp.dot(a, b, preferred_element_type=F32)


def _dot_nt(a, b):
    return lax.dot_general(a, b, (((1,), (1,)), ((), ())), preferred_element_type=F32)


def _rms(x, g):
    ms = jnp.mean(x * x, axis=-1, keepdims=True)
    return x * lax.rsqrt(ms + NORM_EPS) * g


def _softplus(z):
    return jnp.maximum(z, 0.0) + jnp.log(1.0 + jnp.exp(-jnp.abs(z)))


def _split_bf16(x):
    hi = x.astype(BF16)
    lo = (x - hi.astype(F32)).astype(BF16)
    return hi, lo


def _suffix_sums(x, tri):
    hi, lo = _split_bf16(x)
    return _dot(hi, tri) + _dot(lo, tri)


def _tri(n):
    row = lax.broadcasted_iota(jnp.int32, (n, n), 0)
    col = lax.broadcasted_iota(jnp.int32, (n, n), 1)
    return (row >= col).astype(BF16)


def _const_spec(shape):
    zeros = (0,) * len(shape)
    return pl.BlockSpec(shape, lambda *_: zeros, pipeline_mode=pl.Buffered(1))


def _half_step_ffn(x, ffn_refs, act_ref):
    g_pre_ref, wg_ref, wu_ref, wd_ref, g_post_ref = ffn_refs
    h = _rms(x, g_pre_ref[...]).astype(BF16)
    for c in range(wg_ref.shape[1] // FFN_CHUNK):
        cols = slice(c * FFN_CHUNK, (c + 1) * FFN_CHUNK)
        g = _dot(h, wg_ref[:, cols])
        u = _dot(h, wu_ref[:, cols])
        act_ref[:, cols] = ((g * jax.nn.sigmoid(g)) * u).astype(BF16)
    y = _dot(act_ref[...], wd_ref[...])
    return x + FFN_RESIDUAL_WEIGHT * _rms(y, g_post_ref[...])


_IN_Q = 0
_IN_K = SB_WIDTH
_IN_V = 2 * SB_WIDTH
_IN_QD = 3 * SB_WIDTH
_IN_KVD = _IN_QD + Q_LORA
_IN_KPE = _IN_KVD + KV_LORA
_IN_KPE_SWAP = _IN_KPE + LANES
_IN_COLS_EXT = _IN_KPE_SWAP + LANES
_QUP_NOPE = 0
_QUP_PE = MLA_HEADS * MLA_NOPE_DIM
_QUP_PE_SWAP = _QUP_PE + MLA_HEADS * LANES
_QUP_COLS_EXT = _QUP_PE_SWAP + MLA_HEADS * LANES


def _ffn_proj_kernel(x_ref, cos_ref, sin_ref, g_pre_ref, wg_ref, wu_ref, wd_ref, g_post_ref,
                     g_mix_ref, w_in_ref, g_q_ref, w_qup_ref, w_uk_ref, g_kv_ref,
                     x1_ref, sbk_ref, sbv_ref, ckv_ref, kpe_ref,
                     qa_ref, ka_ref, va_ref, qcat_ref, kcat_ref, ckvt_ref, act_ref):
    x1 = _half_step_ffn(x_ref[0], (g_pre_ref, wg_ref, wu_ref, wd_ref, g_post_ref), act_ref)
    x1_ref[0] = x1

    hm = _rms(x1, g_mix_ref[...]).astype(BF16)
    proj = _dot(hm, w_in_ref[...])
    sbk = proj[:, _IN_K:_IN_K + SB_WIDTH]
    sbv = proj[:, _IN_V:_IN_V + SB_WIDTH]
    sbk_ref[0] = sbk.T
    sbv_ref[0] = sbv.T
    for p in range(SB_PAIRS):
        lo, hi = p * LANES, (p + 1) * LANES
        qa_ref[0, p] = proj[:, _IN_Q + lo:_IN_Q + hi].astype(BF16)
        ka_ref[0, p] = sbk[:, lo:hi].astype(BF16)
        va_ref[0, p] = sbv[:, lo:hi].astype(BF16)

    cos = cos_ref[...]
    sin = sin_ref[...]
    qn = _rms(proj[:, _IN_QD:_IN_QD + Q_LORA], g_q_ref[...]).astype(BF16)
    qup = _dot(qn, w_qup_ref[...])
    qnope = qup[:, _QUP_NOPE:_QUP_PE].astype(BF16)
    qlat = jnp.concatenate([_dot(qnope[:, p * LANES:(p + 1) * LANES], w_uk_ref[p])
                            for p in range(MLA_HEADS // 2)], axis=-1)
    for h in range(MLA_HEADS):
        lo, hi = h * LANES, (h + 1) * LANES
        pe = (qup[:, _QUP_PE + lo:_QUP_PE + hi] * cos
              + qup[:, _QUP_PE_SWAP + lo:_QUP_PE_SWAP + hi] * sin)
        qcat_ref[0, h, :, :LANES] = (qlat[:, lo:hi] * MLA_EXP2_SCALE).astype(BF16)
        qcat_ref[0, h, :, LANES:] = (pe * MLA_EXP2_SCALE).astype(BF16)

    ckv = _rms(proj[:, _IN_KVD:_IN_KVD + KV_LORA], g_kv_ref[...])
    kpe = (proj[:, _IN_KPE:_IN_KPE + LANES] * cos
           + proj[:, _IN_KPE_SWAP:_IN_KPE_SWAP + LANES] * sin)
    ckv_ref[0] = ckv
    kpe_ref[0] = kpe[:, :MLA_ROPE_DIM]
    kcat_ref[0, :, :LANES] = ckv.astype(BF16)
    kcat_ref[0, :, LANES:] = kpe.astype(BF16)
    ckvt_ref[0, :KV_LORA, :] = ckv.T.astype(BF16)
    ckvt_ref[0, KV_LORA:, :] = jnp.ones((CKVT_ROWS - KV_LORA, ckv.shape[0]), BF16)


def _ffn_proj_call(x, cos_t, sin_t, ffn_w, proj_w, tm):
    nb, t, d = x.shape
    weights = tuple(ffn_w) + tuple(proj_w)
    d_ff = ffn_w[1].shape[1]
    tok = lambda shape: pl.BlockSpec((1, tm) + shape, lambda b, i: (b, i) + (0,) * len(shape))
    heads = lambda n, w: pl.BlockSpec((1, n, tm, w), lambda b, i: (b, 0, i, 0))
    tab = pl.BlockSpec((tm, LANES), lambda b, i: (i, 0))
    in_specs = [tok((d,)), tab, tab] + [_const_spec(w.shape) for w in weights]
    out_shape = (
        jax.ShapeDtypeStruct((nb, t, d), F32),
        jax.ShapeDtypeStruct((nb, SB_WIDTH, t), F32),
        jax.ShapeDtypeStruct((nb, SB_WIDTH, t), F32),
        jax.ShapeDtypeStruct((nb, t, KV_LORA), F32),
        jax.ShapeDtypeStruct((nb, t, MLA_ROPE_DIM), F32),
        jax.ShapeDtypeStruct((nb, SB_PAIRS, t, LANES), BF16),
        jax.ShapeDtypeStruct((nb, SB_PAIRS, t, LANES), BF16),
        jax.ShapeDtypeStruct((nb, SB_PAIRS, t, LANES), BF16),
        jax.ShapeDtypeStruct((nb, MLA_HEADS, t, QCAT), BF16),
        jax.ShapeDtypeStruct((nb, t, QCAT), BF16),
        jax.ShapeDtypeStruct((nb, CKVT_ROWS, t), BF16),
    )
    tok_minor = lambda n: pl.BlockSpec((1, n, tm), lambda b, i: (b, 0, i))
    out_specs = (tok((d,)), tok_minor(SB_WIDTH), tok_minor(SB_WIDTH), tok((KV_LORA,)), tok((MLA_ROPE_DIM,)),
                 heads(SB_PAIRS, LANES), heads(SB_PAIRS, LANES), heads(SB_PAIRS, LANES),
                 heads(MLA_HEADS, QCAT), tok((QCAT,)), tok_minor(CKVT_ROWS))
    return pl.pallas_call(
        _ffn_proj_kernel,
        grid=(nb, t // tm),
        in_specs=in_specs,
        out_specs=out_specs,
        out_shape=out_shape,
        scratch_shapes=[pltpu.VMEM((tm, d_ff), BF16)],
        compiler_params=pltpu.CompilerParams(
            dimension_semantics=("parallel", "parallel"), vmem_limit_bytes=VMEM_LIMIT_BYTES),
        name="ffn_proj",
    )(x, cos_t, sin_t, *weights)


def _out_ffn_kernel(osb_ref, olat_ref, x1_ref, g_sb_ref, g_mla_ref, w_uv_ref, w_out_ref, g_mixpost_ref,
                    g_pre_ref, wg_ref, wu_ref, wd_ref, g_post_ref, y_ref, act_ref):
    osb = jnp.concatenate([osb_ref[0, p] for p in range(SB_PAIRS)], axis=-1)
    olat = jnp.concatenate([olat_ref[0, h] for h in range(MLA_HEADS)], axis=-1)
    olat = olat.astype(BF16)
    omla = jnp.concatenate([_dot(olat[:, p * 2 * KV_LORA:(p + 1) * 2 * KV_LORA], w_uv_ref[p])
                            for p in range(MLA_HEADS // 2)], axis=-1)
    merged = jnp.concatenate([_rms(osb, g_sb_ref[...]), _rms(omla, g_mla_ref[...])], axis=-1)
    o = _dot(merged.astype(BF16), w_out_ref[...])
    x2 = x1_ref[0] + _rms(o, g_mixpost_ref[...])
    y_ref[0] = _half_step_ffn(x2, (g_pre_ref, wg_ref, wu_ref, wd_ref, g_post_ref), act_ref)


def _out_ffn_call(osb, olat, x1, out_w, ffn_w, tm):
    nb, t, d = x1.shape
    d_ff = ffn_w[1].shape[1]
    heads = lambda n: pl.BlockSpec((1, n, tm, LANES), lambda b, i: (b, 0, i, 0))
    tok = pl.BlockSpec((1, tm, d), lambda b, i: (b, i, 0))
    weights = tuple(out_w) + tuple(ffn_w)
    return pl.pallas_call(
        _out_ffn_kernel,
        grid=(nb, t // tm),
        in_specs=[heads(SB_PAIRS), heads(MLA_HEADS), tok] + [_const_spec(w.shape) for w in weights],
        out_specs=tok,
        out_shape=jax.ShapeDtypeStruct((nb, t, d), F32),
        scratch_shapes=[pltpu.VMEM((tm, d_ff), BF16)],
        compiler_params=pltpu.CompilerParams(
            dimension_semantics=("parallel", "parallel"), vmem_limit_bytes=VMEM_LIMIT_BYTES),
        name="out_ffn",
    )(osb, olat, x1, *weights)


def _prompt_attn_kernel(qa_ref, ka_ref, va_ref, qcat_ref, kcat_ref, ckvt_ref, osb_ref, olat_ref):
    qi = pl.program_id(1)
    blk = ATTN_BLOCK
    tri = _tri(blk)
    rows_sb = lax.broadcasted_iota(jnp.int32, (SB_HEADS * blk, blk), 0)
    strict = lax.broadcasted_iota(jnp.int32, (SB_HEADS * blk, blk), 1) < rows_sb % blk
    even_q = jnp.logical_xor(lax.broadcasted_iota(jnp.int32, (2 * blk, LANES), 1) >= SB_HEAD_DIM,
                             lax.broadcasted_iota(jnp.int32, (2 * blk, LANES), 0) < blk)
    diag = pl.multiple_of(qi * blk, blk)

    def pair_operands(p, off):
        k = ka_ref[0, p, pl.ds(off, blk), :]
        v = va_ref[0, p, pl.ds(off, blk), :]
        v2 = jnp.concatenate([v, v], axis=0)
        return k, jnp.where(even_q, v2, jnp.zeros_like(v2))

    q2s = []
    for p in range(SB_PAIRS):
        qp = qa_ref[0, p]
        qp2 = jnp.concatenate([qp, qp], axis=0)
        q2s.append(jnp.where(even_q, qp2, jnp.zeros_like(qp2)))

    def sb_step(off, carry, accs, masked):
        operands = [pair_operands(p, off) for p in range(SB_PAIRS)]
        z = jnp.concatenate([_dot_nt(q2s[p], operands[p][0]) for p in range(SB_PAIRS)], axis=0)
        sp = _softplus(z)
        if masked:
            sp = jnp.where(strict, sp, 0.0)
        suf = _dot(sp.astype(BF16), tri)
        a = jnp.exp(z - suf - carry)
        if masked:
            a = jnp.where(strict, a, 0.0)
        a = a.astype(BF16)
        out = []
        for p in range(SB_PAIRS):
            a_even = a[2 * p * blk:(2 * p + 1) * blk]
            a_odd = a[(2 * p + 1) * blk:(2 * p + 2) * blk]
            out.append(accs[p] + _dot(jnp.concatenate([a_even, a_odd], axis=1), operands[p][1]))
        return carry + suf[:, 0:1], tuple(out)

    a0 = tuple(jnp.zeros((blk, LANES), F32) for _ in range(SB_PAIRS))
    carry, accs = sb_step(diag, jnp.zeros((SB_HEADS * blk, 1), F32), a0, True)

    def sb_cond(st):
        j, carry, _ = st
        return jnp.logical_and(j >= 0, jnp.min(carry) < SB_EXIT)

    def sb_body(st):
        j, carry, accs = st
        carry, accs = sb_step(pl.multiple_of(j * blk, blk), carry, accs, False)
        return j - 1, carry, accs

    _, _, accs = lax.while_loop(sb_cond, sb_body, (qi - 1, carry, accs))
    for p in range(SB_PAIRS):
        osb_ref[0, p] = accs[p]

    ch = MLA_CHUNK

    cols = MLA_HEADS * blk
    qc = qcat_ref[0].reshape(cols, QCAT)

    def mla_chunk(off, m, acc, masked):
        s = _dot_nt(kcat_ref[0, pl.ds(off, ch), :], qc)
        if masked:
            k_pos = off + lax.broadcasted_iota(jnp.int32, (ch, cols), 0)
            s = jnp.where(k_pos <= diag + lax.broadcasted_iota(jnp.int32, (ch, cols), 1) % blk, s, NEG_INF)
        m_new = jnp.maximum(m, jnp.max(s, axis=0, keepdims=True))
        p = jnp.exp2(s - m_new)
        alpha = jnp.exp2(m - m_new)
        return m_new, alpha * acc + _dot(ckvt_ref[0, :, pl.ds(off, ch)], p.astype(BF16))

    n_full = diag // ch
    st = (jnp.full((1, cols), NEG_INF, F32), jnp.zeros((CKVT_ROWS, cols), F32))
    st = lax.fori_loop(0, n_full, lambda j, st: mla_chunk(pl.multiple_of(j * ch, ch), *st, False), st)
    _, acc = mla_chunk(pl.multiple_of(n_full * ch, ch), *st, True)
    o_t = acc[:KV_LORA] / acc[KV_LORA:KV_LORA + 1]
    for h in range(MLA_HEADS):
        olat_ref[0, h] = o_t[:, h * blk:(h + 1) * blk].T


def _prompt_attn_call(qa, ka, va, qcat, kcat, ckvt):
    nb, _, t, _ = qa.shape
    blk = ATTN_BLOCK
    q_spec = lambda n, w: pl.BlockSpec((1, n, blk, w), lambda b, i: (b, 0, i, 0))
    kv_spec = pl.BlockSpec((1, SB_PAIRS, t, LANES), lambda b, i: (b, 0, 0, 0))
    return pl.pallas_call(
        _prompt_attn_kernel,
        grid=(nb, t // blk),
        in_specs=[q_spec(SB_PAIRS, LANES), kv_spec, kv_spec, q_spec(MLA_HEADS, QCAT),
                  pl.BlockSpec((1, t, QCAT), lambda b, i: (b, 0, 0)),
                  pl.BlockSpec((1, CKVT_ROWS, t), lambda b, i: (b, 0, 0))],
        out_specs=(q_spec(SB_PAIRS, LANES), q_spec(MLA_HEADS, KV_LORA)),
        out_shape=(jax.ShapeDtypeStruct((nb, SB_PAIRS, t, LANES), F32),
                   jax.ShapeDtypeStruct((nb, MLA_HEADS, t, KV_LORA), F32)),
        compiler_params=pltpu.CompilerParams(
            dimension_semantics=("parallel", "parallel"), vmem_limit_bytes=VMEM_LIMIT_BYTES),
        name="prompt_attn",
    )(qa, ka, va, qcat, kcat, ckvt)


def _sb_decode_kernel(pt_ref, q_ref, knew_hbm, vnew_hbm, kc_hbm, vc_hbm, o_ref,
                      kbuf, vbuf, kslow, vslow, sem, slow_sem, *, n_pages, n_new):
    step = pl.program_id(0)
    n_steps = pl.num_programs(0)
    grp = q_ref.shape[0]
    page = kc_hbm.shape[-1]
    first = SBD_PAGES * page
    ntok = first + page

    def first_copies(s, slot):
        out = []
        for g in range(grp):
            b = s * grp + g
            for hbm, new, buf, t in ((kc_hbm, knew_hbm, kbuf, 0), (vc_hbm, vnew_hbm, vbuf, 1)):
                for j in range(SBD_PAGES):
                    pid = pt_ref[b * n_pages + (n_pages - SBD_PAGES + j)]
                    out.append(pltpu.make_async_copy(
                        hbm.at[pid], buf.at[slot * grp + g, :, :, pl.ds(j * page, page)], sem.at[slot, t]))
                tile = pl.multiple_of((b * n_new) // page * page, page)
                out.append(pltpu.make_async_copy(
                    new.at[:, :, pl.ds(tile, page)], buf.at[slot * grp + g, :, :, pl.ds(first, page)],
                    sem.at[slot, t]))
        return out

    @pl.when(step == 0)
    def _():
        for c in first_copies(0, 0):
            c.start()

    slot = step % 2

    @pl.when(step + 1 < n_steps)
    def _():
        for c in first_copies(step + 1, 1 - slot):
            c.start()

    for c in first_copies(step, slot):
        c.wait()

    rows = SB_HEADS * SUBLANES
    r_iota = lax.broadcasted_iota(jnp.int32, (rows, ntok), 0)
    c_iota = lax.broadcasted_iota(jnp.int32, (rows, ntok), 1)
    real_row = (lax.broadcasted_iota(jnp.int32, (rows, 1), 0) % SUBLANES) < n_new
    tri = _tri(ntok)
    tri_slow = _tri(first)

    def scores(qs, buf, idx):
        return jnp.concatenate([_dot(qs[h], buf[idx, h].astype(BF16)) for h in range(SB_HEADS)], axis=0)

    def weighted(a, buf, idx):
        v_all = buf[idx].reshape(SB_WIDTH, -1).astype(BF16)
        full = _dot_nt(a, v_all)
        return jnp.concatenate(
            [full[h * SUBLANES:(h + 1) * SUBLANES, h * SB_HEAD_DIM:(h + 1) * SB_HEAD_DIM]
             for h in range(SB_HEADS)], axis=0)

    def newest_block(g):
        idx = slot * grp + g
        j_new = c_iota - (first + ((step * grp + g) * n_new) % page)
        valid = jnp.logical_or(c_iota < first, jnp.logical_and(j_new >= 0, j_new < r_iota % SUBLANES))
        qs = [q_ref[g, h].astype(BF16) for h in range(SB_HEADS)]
        z = scores(qs, kbuf, idx)
        sp = jnp.where(valid, _softplus(z), 0.0)
        suf = _suffix_sums(sp, tri)
        a = jnp.where(valid, jnp.exp(z - suf), 0.0).astype(BF16)
        return qs, suf[:, 0:1], weighted(a, vbuf, idx)

    newest = [newest_block(g) for g in range(grp)]

    for g in range(grp):
        b = step * grp + g
        qs, carry, acc = newest[g]

        def cond(st):
            c, carry, _ = st
            live = jnp.min(jnp.where(real_row, carry, SB_EXIT)) < SB_EXIT
            return jnp.logical_and(c >= 0, live)

        def body(st):
            c, carry, acc = st
            copies = []
            for hbm, buf, t in ((kc_hbm, kslow, 0), (vc_hbm, vslow, 1)):
                for j in range(SBD_PAGES):
                    pid = pt_ref[b * n_pages + c * SBD_PAGES + j]
                    copies.append(pltpu.make_async_copy(
                        hbm.at[pid], buf.at[0, :, :, pl.ds(j * page, page)], slow_sem.at[t]))
            for cp in copies:
                cp.start()
            for cp in copies:
                cp.wait()
            z = scores(qs, kslow, 0)
            suf = _suffix_sums(_softplus(z), tri_slow)
            a = jnp.exp(z - suf - carry).astype(BF16)
            return c - 1, carry + suf[:, 0:1], acc + weighted(a, vslow, 0)

        _, _, acc = lax.while_loop(cond, body, (n_pages // SBD_PAGES - 2, carry, acc))
        o_ref[g] = acc.reshape(SB_HEADS, SUBLANES, SB_HEAD_DIM)


def _sb_decode_call(pt_flat, q8, knew_t, vnew_t, cache_kt, cache_vt, n_pages, n_new):
    nbatch = q8.shape[0]
    page = cache_kt.shape[-1]
    grp = SBD_GROUP
    ntok = (SBD_PAGES + 1) * page
    any_spec = pl.BlockSpec(memory_space=pl.ANY)
    blk = pl.BlockSpec((grp, SB_HEADS, SUBLANES, SB_HEAD_DIM), lambda s, pt: (s, 0, 0, 0))
    kernel = functools.partial(_sb_decode_kernel, n_pages=n_pages, n_new=n_new)
    return pl.pallas_call(
        kernel,
        grid_spec=pltpu.PrefetchScalarGridSpec(
            num_scalar_prefetch=1,
            grid=(nbatch // grp,),
            in_specs=[blk, any_spec, any_spec, any_spec, any_spec],
            out_specs=blk,
            scratch_shapes=[
                pltpu.VMEM((2 * grp, SB_HEADS, SB_HEAD_DIM, ntok), F32),
                pltpu.VMEM((2 * grp, SB_HEADS, SB_HEAD_DIM, ntok), F32),
                pltpu.VMEM((1, SB_HEADS, SB_HEAD_DIM, SBD_PAGES * page), F32),
                pltpu.VMEM((1, SB_HEADS, SB_HEAD_DIM, SBD_PAGES * page), F32),
                pltpu.SemaphoreType.DMA((2, 2)),
                pltpu.SemaphoreType.DMA((2,)),
            ]),
        out_shape=jax.ShapeDtypeStruct((nbatch, SB_HEADS, SUBLANES, SB_HEAD_DIM), F32),
        compiler_params=pltpu.CompilerParams(
            dimension_semantics=("arbitrary",), vmem_limit_bytes=VMEM_LIMIT_BYTES),
        name="sb_decode",
    )(pt_flat, q8, knew_t, vnew_t, cache_kt, cache_vt)


def _mla_decode_kernel(pt_ref, q_ref, ckvnew_ref, kpenew_ref, ckv_hbm, kpe_hbm, o_ref,
                       ckvbuf, kpebuf, m_ref, l_ref, acc_ref, sem, *, n_pages):
    b = pl.program_id(0)
    c = pl.program_id(1)
    n_chunks = pl.num_programs(1)
    n = b * n_chunks + c
    total = pl.num_programs(0) * n_chunks
    page = ckv_hbm.shape[1]
    chunk_pages = ckvbuf.shape[1] // page

    def copies(i, slot):
        bb = i // n_chunks
        cc = i % n_chunks
        out = []
        for j in range(chunk_pages):
            pid = pt_ref[bb * n_pages + cc * chunk_pages + j]
            out.append(pltpu.make_async_copy(
                ckv_hbm.at[pid], ckvbuf.at[slot, pl.ds(j * page, page), :], sem.at[slot, 0]))
            out.append(pltpu.make_async_copy(
                kpe_hbm.at[pid], kpebuf.at[slot, :, pl.ds(j * page, page)], sem.at[slot, 1]))
        return out

    @pl.when(n == 0)
    def _():
        for cp in copies(0, 0):
            cp.start()

    slot = n % 2

    @pl.when(n + 1 < total)
    def _():
        for cp in copies(n + 1, 1 - slot):
            cp.start()

    q = q_ref[0]
    q_lat = q[:, :KV_LORA]
    q_pe = q[:, KV_LORA:KV_LORA + MLA_ROPE_DIM]

    def scores(ckv, kpe_t):
        return _dot_nt(q_lat, ckv) + _dot(q_pe, kpe_t)

    @pl.when(c == 0)
    def _():
        ckv = ckvnew_ref[0].astype(BF16)
        s = scores(ckv, kpenew_ref[0].astype(BF16))
        r_iota = lax.broadcasted_iota(jnp.int32, s.shape, 0)
        c_iota = lax.broadcasted_iota(jnp.int32, s.shape, 1)
        s = jnp.where(c_iota <= r_iota // MLA_HEADS, s, NEG_INF)
        m = jnp.max(s, axis=-1, keepdims=True)
        p = jnp.exp2(s - m)
        m_ref[...] = m
        l_ref[...] = jnp.sum(p, axis=-1, keepdims=True)
        acc_ref[...] = _dot(p.astype(BF16), ckv)

    for cp in copies(n, slot):
        cp.wait()

    ckv = ckvbuf[slot].astype(BF16)
    s = scores(ckv, kpebuf[slot].astype(BF16))
    m_old = m_ref[...]
    m_new = jnp.maximum(m_old, jnp.max(s, axis=-1, keepdims=True))
    p = jnp.exp2(s - m_new)
    alpha = jnp.exp2(m_old - m_new)
    l_new = alpha * l_ref[...] + jnp.sum(p, axis=-1, keepdims=True)
    acc_new = alpha * acc_ref[...] + _dot(p.astype(BF16), ckv)
    m_ref[...] = m_new
    l_ref[...] = l_new
    acc_ref[...] = acc_new

    @pl.when(c == n_chunks - 1)
    def _():
        o_ref[0] = acc_new / l_new


def _mla_decode_call(pt_flat, qcs, ckv_new, kpe_new_t, cache_ckv, cache_kpe_t, n_pages):
    nbatch, rows, _ = qcs.shape
    page = cache_ckv.shape[1]
    chunk_pages = min(MLAD_PAGES, n_pages)
    assert n_pages % chunk_pages == 0
    chunk = chunk_pages * page
    any_spec = pl.BlockSpec(memory_space=pl.ANY)
    per_b = lambda shape: pl.BlockSpec((1,) + shape, lambda b, c, pt: (b, 0, 0))
    kernel = functools.partial(_mla_decode_kernel, n_pages=n_pages)
    return pl.pallas_call(
        kernel,
        grid_spec=pltpu.PrefetchScalarGridSpec(
            num_scalar_prefetch=1,
            grid=(nbatch, n_pages // chunk_pages),
            in_specs=[per_b((rows, QCAT)), per_b((page, KV_LORA)), per_b((MLA_ROPE_DIM, page)),
                      any_spec, any_spec],
            out_specs=per_b((rows, KV_LORA)),
            scratch_shapes=[
                pltpu.VMEM((2, chunk, KV_LORA), F32),
                pltpu.VMEM((2, MLA_ROPE_DIM, chunk), F32),
                pltpu.VMEM((rows, 1), F32),
                pltpu.VMEM((rows, 1), F32),
                pltpu.VMEM((rows, KV_LORA), F32),
                pltpu.SemaphoreType.DMA((2, 2)),
            ]),
        out_shape=jax.ShapeDtypeStruct((nbatch, rows, KV_LORA), F32),
        compiler_params=pltpu.CompilerParams(
            dimension_semantics=("arbitrary", "arbitrary"), vmem_limit_bytes=VMEM_LIMIT_BYTES),
        name="mla_decode",
    )(pt_flat, qcs, ckv_new, kpe_new_t, cache_ckv, cache_kpe_t)


def _prep_ffn(g_pre, w_gate, w_up, w_down, g_post):
    d, d_ff = w_gate.shape
    assert d_ff % FFN_CHUNK == 0
    return (g_pre.reshape(1, d), w_gate.astype(BF16), w_up.astype(BF16), w_down.astype(BF16),
            g_post.reshape(1, d))


def _prep_proj(g_mix_pre, w_in, g_q_lora, w_q_up, w_uk, g_kv_lora):
    d = w_in.shape[0]
    half = MLA_ROPE_DIM // 2
    kpe_w = w_in[:, _IN_KPE:_IN_KPE + MLA_ROPE_DIM]
    pad = jnp.zeros((d, LANES - MLA_ROPE_DIM), w_in.dtype)
    w_in_ext = jnp.concatenate([
        w_in[:, :SB_WIDTH] * SB_SCALE, w_in[:, SB_WIDTH:_IN_KPE],
        kpe_w, pad, kpe_w[:, half:], kpe_w[:, :half], pad], axis=1).astype(BF16)

    wq = w_q_up.reshape(Q_LORA, MLA_HEADS, MLA_QK_DIM)
    nope = wq[:, :, :MLA_NOPE_DIM].reshape(Q_LORA, MLA_HEADS * MLA_NOPE_DIM)
    pe = wq[:, :, MLA_NOPE_DIM:]
    pe_swap = jnp.concatenate([pe[:, :, half:], pe[:, :, :half]], axis=-1)
    lane_pad = lambda a: jnp.pad(a, ((0, 0), (0, 0), (0, LANES - MLA_ROPE_DIM))).reshape(Q_LORA, MLA_HEADS * LANES)
    w_qup_ext = jnp.concatenate([nope, lane_pad(pe), lane_pad(pe_swap)], axis=1).astype(BF16)

    eye = jnp.eye(MLA_HEADS, dtype=w_uk.dtype)
    w_uk_bd = jnp.einsum('chn,hg->hngc', w_uk, eye).reshape(MLA_HEADS * MLA_NOPE_DIM, MLA_HEADS * KV_LORA)
    return (g_mix_pre.reshape(1, d), w_in_ext, g_q_lora.reshape(1, Q_LORA), w_qup_ext,
            _pair_blocks(w_uk_bd).astype(BF16), g_kv_lora.reshape(1, KV_LORA))


def _pair_blocks(w_bd):
    n = MLA_HEADS // 2
    r, c = w_bd.shape[0] // n, w_bd.shape[1] // n
    return jnp.stack([w_bd[p * r:(p + 1) * r, p * c:(p + 1) * c] for p in range(n)])


def _prep_out(g_sb_out, g_mla_out, w_uv, w_out, g_mix_post):
    eye = jnp.eye(MLA_HEADS, dtype=w_uv.dtype)
    w_uv_bd = jnp.einsum('chv,hg->hcgv', w_uv, eye).reshape(MLA_HEADS * KV_LORA, MLA_WIDTH)
    return (g_sb_out.reshape(1, SB_WIDTH), g_mla_out.reshape(1, MLA_WIDTH), _pair_blocks(w_uv_bd).astype(BF16),
            w_out.astype(BF16), g_mix_post.reshape(1, -1))


def _rope_tables(pos):
    half = MLA_ROPE_DIM // 2
    inv_freq = ROPE_THETA ** (-jnp.arange(half, dtype=F32) / half)
    ang = pos.astype(F32)[:, None] * inv_freq[None, :]
    cos, sin = jnp.cos(ang), jnp.sin(ang)
    pad = jnp.zeros((pos.shape[0], LANES - MLA_ROPE_DIM), F32)
    return (jnp.concatenate([cos, cos, pad], axis=1), jnp.concatenate([-sin, sin, pad], axis=1))


def _token_tile(t):
    for tm in (256, 128):
        if t % tm == 0:
            return tm
    raise ValueError(f"token count {t} is not a multiple of 128")


def kernel(x_prompt, x_sample, cache_sb_k, cache_sb_v, cache_mla_ckv, cache_mla_kpe, page_table,
           g_ffn1_pre, w_ffn1_gate, w_ffn1_up, w_ffn1_down, g_ffn1_post,
           g_mix_pre, w_in, g_q_lora, w_q_up, g_kv_lora, w_uk, w_uv, g_sb_out, g_mla_out, w_out, g_mix_post,
           g_ffn2_pre, w_ffn2_gate, w_ffn2_up, w_ffn2_down, g_ffn2_post):
    batch, seq, d = x_prompt.shape
    dec_batch, dec_seq, _ = x_sample.shape
    page = cache_sb_k.shape[1]
    n_pages = page_table.shape[1]
    past_len = n_pages * page
    assert seq % MLA_CHUNK == 0 and MLA_CHUNK % ATTN_BLOCK == 0 and dec_seq <= SUBLANES
    assert n_pages % SBD_PAGES == 0 and n_pages >= 2 * SBD_PAGES
    assert dec_batch % SBD_GROUP == 0 and page % dec_seq == 0 and (dec_batch * dec_seq) % page == 0

    ffn1 = _prep_ffn(g_ffn1_pre, w_ffn1_gate, w_ffn1_up, w_ffn1_down, g_ffn1_post)
    ffn2 = _prep_ffn(g_ffn2_pre, w_ffn2_gate, w_ffn2_up, w_ffn2_down, g_ffn2_post)
    proj_w = _prep_proj(g_mix_pre, w_in, g_q_lora, w_q_up, w_uk, g_kv_lora)
    out_w = _prep_out(g_sb_out, g_mla_out, w_uv, w_out, g_mix_post)

    cos_p, sin_p = _rope_tables(jnp.arange(seq))
    tm_p = _token_tile(seq)
    (x1_p, sbk_p, sbv_p, ckv_p, kpe_p, qa_p, ka_p, va_p, qcat_p, kcat_p, ckvt_p) = _ffn_proj_call(
        x_prompt, cos_p, sin_p, ffn1, proj_w, tm_p)
    osb_p, olat_p = _prompt_attn_call(qa_p, ka_p, va_p, qcat_p, kcat_p, ckvt_p)
    y_p = _out_ffn_call(osb_p, olat_p, x1_p, out_w, ffn2, tm_p)

    n_tok = dec_batch * dec_seq
    cos_s, sin_s = _rope_tables(past_len + (jnp.arange(n_tok) % dec_seq))
    tm_s = _token_tile(n_tok)
    (x1_s, sbk_s, sbv_s, ckv_s, kpe_s, qa_s, _, _, qcat_s, _, _) = _ffn_proj_call(
        x_sample.reshape(1, n_tok, d), cos_s, sin_s, ffn1, proj_w, tm_s)

    pt_flat = page_table.reshape(-1)
    pad_tok = page - dec_seq
    q8 = qa_s[0].transpose(1, 0, 2).reshape(dec_batch, dec_seq, SB_HEADS, SB_HEAD_DIM).transpose(0, 2, 1, 3)
    q8 = jnp.pad(q8, ((0, 0), (0, 0), (0, SUBLANES - dec_seq), (0, 0))).astype(F32)
    head_dim_tok = lambda a: a.reshape(SB_HEADS, SB_HEAD_DIM, n_tok)
    o8 = _sb_decode_call(pt_flat, q8, head_dim_tok(sbk_s), head_dim_tok(sbv_s),
                         cache_sb_k.transpose(0, 2, 3, 1), cache_sb_v.transpose(0, 2, 3, 1), n_pages, dec_seq)
    osb_s = o8[:, :, :dec_seq, :].transpose(0, 2, 1, 3).reshape(n_tok, SB_PAIRS, LANES).transpose(1, 0, 2)[None]

    qcs = qcat_s[0].reshape(MLA_HEADS, dec_batch, dec_seq, QCAT).transpose(1, 2, 0, 3)
    qcs = qcs.reshape(dec_batch, dec_seq * MLA_HEADS, QCAT)
    ckv_new = jnp.pad(ckv_s.reshape(dec_batch, dec_seq, KV_LORA), ((0, 0), (0, pad_tok), (0, 0)))
    kpe_new_t = jnp.pad(kpe_s.reshape(dec_batch, dec_seq, MLA_ROPE_DIM).transpose(0, 2, 1),
                        ((0, 0), (0, 0), (0, pad_tok)))
    olat = _mla_decode_call(pt_flat, qcs, ckv_new, kpe_new_t, cache_mla_ckv,
                            cache_mla_kpe.transpose(0, 2, 1), n_pages)
    olat_s = olat.reshape(dec_batch, dec_seq, MLA_HEADS, KV_LORA).transpose(2, 0, 1, 3)
    olat_s = olat_s.reshape(1, MLA_HEADS, n_tok, KV_LORA)

    y_s = _out_ffn_call(osb_s, olat_s, x1_s, out_w, ffn2, tm_s)

    tok_head_dim = lambda a, lead: a.reshape(lead + (SB_HEADS, SB_HEAD_DIM, -1)).transpose(
        tuple(range(len(lead))) + (len(lead) + 2, len(lead), len(lead) + 1))
    return (y_p, y_s.reshape(dec_batch, dec_seq, d),
            tok_head_dim(sbk_p, (batch,)), tok_head_dim(sbv_p, (batch,)),
            ckv_p, kpe_p,
            tok_head_dim(sbk_s[0], ()).reshape(dec_batch, dec_seq, SB_HEADS, SB_HEAD_DIM),
            tok_head_dim(sbv_s[0], ()).reshape(dec_batch, dec_seq, SB_HEADS, SB_HEAD_DIM),
            ckv_s.reshape(dec_batch, dec_seq, KV_LORA), kpe_s.reshape(dec_batch, dec_seq, MLA_ROPE_DIM))
```

```python
import functools

import jax
import jax.numpy as jnp
from jax import lax
from jax.experimental import pallas as pl
from jax.experimental.pallas import tpu as pltpu

F32 = jnp.float32
BF16 = jnp.bfloat16

SB_HEADS = 8
SB_HEAD_DIM = 64
SB_WIDTH = SB_HEADS * SB_HEAD_DIM
SB_PAIRS = SB_HEADS // 2
MLA_HEADS = 8
MLA_NOPE_DIM = 64
MLA_ROPE_DIM = 32
MLA_V_DIM = 64
MLA_QK_DIM = MLA_NOPE_DIM + MLA_ROPE_DIM
MLA_WIDTH = MLA_HEADS * MLA_V_DIM
Q_LORA = 256
KV_LORA = 128
FFN_RESIDUAL_WEIGHT = 0.5
ROPE_THETA = 10000.0
NORM_EPS = 1e-6
SB_SCALE = SB_HEAD_DIM ** -0.5
MLA_SCALE = MLA_QK_DIM ** -0.5
LOG2_E = 1.4426950408889634
MLA_EXP2_SCALE = MLA_SCALE * LOG2_E
NEG_INF = -1e30

LANES = 128
SUBLANES = 8
MXU_DIM = 256
VMEM_LIMIT_BYTES = 56 * 1024 * 1024

FFN_CHUNK = MXU_DIM
ATTN_BLOCK = MXU_DIM
MLA_CHUNK = 2 * MXU_DIM
CKVT_ROWS = KV_LORA + 16
QCAT = 2 * LANES
SB_EXIT = 104.0

SBD_GROUP = 4
SBD_PAGES = 2
MLAD_PAGES = 64


def _dot(a, b):
    return jnp.dot(a, b, preferred_element_type=F32)


def _dot_nt(a, b):
    return lax.dot_general(a, b, (((1,), (1,)), ((), ())), preferred_element_type=F32)


def _rms(x, g):
    ms = jnp.mean(x * x, axis=-1, keepdims=True)
    return x * lax.rsqrt(ms + NORM_EPS) * g


def _softplus(z):
    return jnp.maximum(z, 0.0) + jnp.log(1.0 + jnp.exp(-jnp.abs(z)))


def _split_bf16(x):
    hi = x.astype(BF16)
    lo = (x - hi.astype(F32)).astype(BF16)
    return hi, lo


def _suffix_sums(x, tri):
    hi, lo = _split_bf16(x)
    return _dot(hi, tri) + _dot(lo, tri)


def _tri(n):
    row = lax.broadcasted_iota(jnp.int32, (n, n), 0)
    col = lax.broadcasted_iota(jnp.int32, (n, n), 1)
    return (row >= col).astype(BF16)


def _const_spec(shape):
    zeros = (0,) * len(shape)
    return pl.BlockSpec(shape, lambda *_: zeros, pipeline_mode=pl.Buffered(1))


def _half_step_ffn(x, ffn_refs, act_ref):
    g_pre_ref, wg_ref, wu_ref, wd_ref, g_post_ref = ffn_refs
    h = _rms(x, g_pre_ref[...]).astype(BF16)
    for c in range(wg_ref.shape[1] // FFN_CHUNK):
        cols = slice(c * FFN_CHUNK, (c + 1) * FFN_CHUNK)
        g = _dot(h, wg_ref[:, cols])
        u = _dot(h, wu_ref[:, cols])
        act_ref[:, cols] = ((g * jax.nn.sigmoid(g)) * u).astype(BF16)
    y = _dot(act_ref[...], wd_ref[...])
    return x + FFN_RESIDUAL_WEIGHT * _rms(y, g_post_ref[...])


_IN_Q = 0
_IN_K = SB_WIDTH
_IN_V = 2 * SB_WIDTH
_IN_QD = 3 * SB_WIDTH
_IN_KVD = _IN_QD + Q_LORA
_IN_KPE = _IN_KVD + KV_LORA
_IN_KPE_SWAP = _IN_KPE + LANES
_IN_COLS_EXT = _IN_KPE_SWAP + LANES
_QUP_NOPE = 0
_QUP_PE = MLA_HEADS * MLA_NOPE_DIM
_QUP_PE_SWAP = _QUP_PE + MLA_HEADS * LANES
_QUP_COLS_EXT = _QUP_PE_SWAP + MLA_HEADS * LANES


def _ffn_proj_kernel(x_ref, cos_ref, sin_ref, g_pre_ref, wg_ref, wu_ref, wd_ref, g_post_ref,
                     g_mix_ref, w_in_ref, g_q_ref, w_qup_ref, w_uk_ref, g_kv_ref,
                     x1_ref, sbk_ref, sbv_ref, ckv_ref, kpe_ref,
                     qa_ref, ka_ref, va_ref, qcat_ref, kcat_ref, ckvt_ref, act_ref):
    x1 = _half_step_ffn(x_ref[0], (g_pre_ref, wg_ref, wu_ref, wd_ref, g_post_ref), act_ref)
    x1_ref[0] = x1

    hm = _rms(x1, g_mix_ref[...]).astype(BF16)
    proj = _dot(hm, w_in_ref[...])
    sbk = proj[:, _IN_K:_IN_K + SB_WIDTH]
    sbv = proj[:, _IN_V:_IN_V + SB_WIDTH]
    sbk_ref[0] = sbk.T
    sbv_ref[0] = sbv.T
    for p in range(SB_PAIRS):
        lo, hi = p * LANES, (p + 1) * LANES
        qa_ref[0, p] = proj[:, _IN_Q + lo:_IN_Q + hi].astype(BF16)
        ka_ref[0, p] = sbk[:, lo:hi].astype(BF16)
        va_ref[0, p] = sbv[:, lo:hi].astype(BF16)

    cos = cos_ref[...]
    sin = sin_ref[...]
    qn = _rms(proj[:, _IN_QD:_IN_QD + Q_LORA], g_q_ref[...]).astype(BF16)
    qup = _dot(qn, w_qup_ref[...])
    qnope = qup[:, _QUP_NOPE:_QUP_PE].astype(BF16)
    qlat = jnp.concatenate([_dot(qnope[:, p * LANES:(p + 1) * LANES], w_uk_ref[p])
                            for p in range(MLA_HEADS // 2)], axis=-1)
    for h in range(MLA_HEADS):
        lo, hi = h * LANES, (h + 1) * LANES
        pe = (qup[:, _QUP_PE + lo:_QUP_PE + hi] * cos
              + qup[:, _QUP_PE_SWAP + lo:_QUP_PE_SWAP + hi] * sin)
        qcat_ref[0, h, :, :LANES] = (qlat[:, lo:hi] * MLA_EXP2_SCALE).astype(BF16)
        qcat_ref[0, h, :, LANES:] = (pe * MLA_EXP2_SCALE).astype(BF16)

    ckv = _rms(proj[:, _IN_KVD:_IN_KVD + KV_LORA], g_kv_ref[...])
    kpe = (proj[:, _IN_KPE:_IN_KPE + LANES] * cos
           + proj[:, _IN_KPE_SWAP:_IN_KPE_SWAP + LANES] * sin)
    ckv_ref[0] = ckv
    kpe_ref[0] = kpe[:, :MLA_ROPE_DIM]
    kcat_ref[0, :, :LANES] = ckv.astype(BF16)
    kcat_ref[0, :, LANES:] = kpe.astype(BF16)
    ckvt_ref[0, :KV_LORA, :] = ckv.T.astype(BF16)
    ckvt_ref[0, KV_LORA:, :] = jnp.ones((CKVT_ROWS - KV_LORA, ckv.shape[0]), BF16)


def _ffn_proj_call(x, cos_t, sin_t, ffn_w, proj_w, tm):
    nb, t, d = x.shape
    weights = tuple(ffn_w) + tuple(proj_w)
    d_ff = ffn_w[1].shape[1]
    tok = lambda shape: pl.BlockSpec((1, tm) + shape, lambda b, i: (b, i) + (0,) * len(shape))
    heads = lambda n, w: pl.BlockSpec((1, n, tm, w), lambda b, i: (b, 0, i, 0))
    tab = pl.BlockSpec((tm, LANES), lambda b, i: (i, 0))
    in_specs = [tok((d,)), tab, tab] + [_const_spec(w.shape) for w in weights]
    out_shape = (
        jax.ShapeDtypeStruct((nb, t, d), F32),
        jax.ShapeDtypeStruct((nb, SB_WIDTH, t), F32),
        jax.ShapeDtypeStruct((nb, SB_WIDTH, t), F32),
        jax.ShapeDtypeStruct((nb, t, KV_LORA), F32),
        jax.ShapeDtypeStruct((nb, t, MLA_ROPE_DIM), F32),
        jax.ShapeDtypeStruct((nb, SB_PAIRS, t, LANES), BF16),
        jax.ShapeDtypeStruct((nb, SB_PAIRS, t, LANES), BF16),
        jax.ShapeDtypeStruct((nb, SB_PAIRS, t, LANES), BF16),
        jax.ShapeDtypeStruct((nb, MLA_HEADS, t, QCAT), BF16),
        jax.ShapeDtypeStruct((nb, t, QCAT), BF16),
        jax.ShapeDtypeStruct((nb, CKVT_ROWS, t), BF16),
    )
    tok_minor = lambda n: pl.BlockSpec((1, n, tm), lambda b, i: (b, 0, i))
    out_specs = (tok((d,)), tok_minor(SB_WIDTH), tok_minor(SB_WIDTH), tok((KV_LORA,)), tok((MLA_ROPE_DIM,)),
                 heads(SB_PAIRS, LANES), heads(SB_PAIRS, LANES), heads(SB_PAIRS, LANES),
                 heads(MLA_HEADS, QCAT), tok((QCAT,)), tok_minor(CKVT_ROWS))
    return pl.pallas_call(
        _ffn_proj_kernel,
        grid=(nb, t // tm),
        in_specs=in_specs,
        out_specs=out_specs,
        out_shape=out_shape,
        scratch_shapes=[pltpu.VMEM((tm, d_ff), BF16)],
        compiler_params=pltpu.CompilerParams(
            dimension_semantics=("parallel", "parallel"), vmem_limit_bytes=VMEM_LIMIT_BYTES),
        name="ffn_proj",
    )(x, cos_t, sin_t, *weights)


def _out_ffn_kernel(osb_ref, olat_ref, x1_ref, g_sb_ref, g_mla_ref, w_uv_ref, w_out_ref, g_mixpost_ref,
                    g_pre_ref, wg_ref, wu_ref, wd_ref, g_post_ref, y_ref, act_ref):
    osb = jnp.concatenate([osb_ref[0, p] for p in range(SB_PAIRS)], axis=-1)
    olat = jnp.concatenate([olat_ref[0, h] for h in range(MLA_HEADS)], axis=-1)
    olat = olat.astype(BF16)
    omla = jnp.concatenate([_dot(olat[:, p * 2 * KV_LORA:(p + 1) * 2 * KV_LORA], w_uv_ref[p])
                            for p in range(MLA_HEADS // 2)], axis=-1)
    merged = jnp.concatenate([_rms(osb, g_sb_ref[...]), _rms(omla, g_mla_ref[...])], axis=-1)
    o = _dot(merged.astype(BF16), w_out_ref[...])
    x2 = x1_ref[0] + _rms(o, g_mixpost_ref[...])
    y_ref[0] = _half_step_ffn(x2, (g_pre_ref, wg_ref, wu_ref, wd_ref, g_post_ref), act_ref)


def _out_ffn_call(osb, olat, x1, out_w, ffn_w, tm):
    nb, t, d = x1.shape
    d_ff = ffn_w[1].shape[1]
    heads = lambda n: pl.BlockSpec((1, n, tm, LANES), lambda b, i: (b, 0, i, 0))
    tok = pl.BlockSpec((1, tm, d), lambda b, i: (b, i, 0))
    weights = tuple(out_w) + tuple(ffn_w)
    return pl.pallas_call(
        _out_ffn_kernel,
        grid=(nb, t // tm),
        in_specs=[heads(SB_PAIRS), heads(MLA_HEADS), tok] + [_const_spec(w.shape) for w in weights],
        out_specs=tok,
        out_shape=jax.ShapeDtypeStruct((nb, t, d), F32),
        scratch_shapes=[pltpu.VMEM((tm, d_ff), BF16)],
        compiler_params=pltpu.CompilerParams(
            dimension_semantics=("parallel", "parallel"), vmem_limit_bytes=VMEM_LIMIT_BYTES),
        name="out_ffn",
    )(osb, olat, x1, *weights)


def _prompt_attn_kernel(qa_ref, ka_ref, va_ref, qcat_ref, kcat_ref, ckvt_ref, osb_ref, olat_ref,
                        sa_ref, sb_ref, m_ref, acc_ref):
    qi = pl.program_id(1)
    blk = ATTN_BLOCK
    tri = _tri(blk)
    rows_sb = lax.broadcasted_iota(jnp.int32, (SB_HEADS * blk, blk), 0)
    strict = lax.broadcasted_iota(jnp.int32, (SB_HEADS * blk, blk), 1) < rows_sb % blk
    even_q = jnp.logical_xor(lax.broadcasted_iota(jnp.int32, (2 * blk, LANES), 1) >= SB_HEAD_DIM,
                             lax.broadcasted_iota(jnp.int32, (2 * blk, LANES), 0) < blk)
    diag = pl.multiple_of(qi * blk, blk)

    def pair_operands(p, off):
        k = ka_ref[0, p, pl.ds(off, blk), :]
        v = va_ref[0, p, pl.ds(off, blk), :]
        v2 = jnp.concatenate([v, v], axis=0)
        return k, jnp.where(even_q, v2, jnp.zeros_like(v2))

    q2s = []
    for p in range(SB_PAIRS):
        qp = qa_ref[0, p]
        qp2 = jnp.concatenate([qp, qp], axis=0)
        q2s.append(jnp.where(even_q, qp2, jnp.zeros_like(qp2)))

    def sb_step(off, carry, accs, masked):
        operands = [pair_operands(p, off) for p in range(SB_PAIRS)]
        z = jnp.concatenate([_dot_nt(q2s[p], operands[p][0]) for p in range(SB_PAIRS)], axis=0)
        sp = _softplus(z)
        if masked:
            sp = jnp.where(strict, sp, 0.0)
        suf = _dot(sp.astype(BF16), tri)
        a = jnp.exp(z - suf - carry)
        if masked:
            a = jnp.where(strict, a, 0.0)
        a = a.astype(BF16)
        out = []
        for p in range(SB_PAIRS):
            a_even = a[2 * p * blk:(2 * p + 1) * blk]
            a_odd = a[(2 * p + 1) * blk:(2 * p + 2) * blk]
            out.append(accs[p] + _dot(jnp.concatenate([a_even, a_odd], axis=1), operands[p][1]))
        return carry + suf[:, 0:1], tuple(out)

    a0 = tuple(jnp.zeros((blk, LANES), F32) for _ in range(SB_PAIRS))
    carry, accs = sb_step(diag, jnp.zeros((SB_HEADS * blk, 1), F32), a0, True)

    def sb_cond(st):
        j, carry, _ = st
        return jnp.logical_and(j >= 0, jnp.min(carry) < SB_EXIT)

    def sb_body(st):
        j, carry, accs = st
        carry, accs = sb_step(pl.multiple_of(j * blk, blk), carry, accs, False)
        return j - 1, carry, accs

    _, _, accs = lax.while_loop(sb_cond, sb_body, (qi - 1, carry, accs))
    for p in range(SB_PAIRS):
        osb_ref[0, p] = accs[p]

    ch = MLA_CHUNK

    cols = MLA_HEADS * blk
    qc = qcat_ref[0].reshape(cols, QCAT)

    def scores_into(dst_ref, j):
        off = pl.multiple_of(j * ch, ch)
        dst_ref[...] = _dot_nt(kcat_ref[0, pl.ds(off, ch), :], qc)

    def consume(src_ref, j, masked):
        off = pl.multiple_of(j * ch, ch)
        s = src_ref[...]
        if masked:
            k_pos = off + lax.broadcasted_iota(jnp.int32, (ch, cols), 0)
            s = jnp.where(k_pos <= diag + lax.broadcasted_iota(jnp.int32, (ch, cols), 1) % blk, s, NEG_INF)
        m = m_ref[...]
        m_new = jnp.maximum(m, jnp.max(s, axis=0, keepdims=True))
        p = jnp.exp2(s - m_new)
        acc_ref[...] = jnp.exp2(m - m_new) * acc_ref[...] + _dot(ckvt_ref[0, :, pl.ds(off, ch)], p.astype(BF16))
        m_ref[...] = m_new

    n_full = diag // ch
    m_ref[...] = jnp.full((1, cols), NEG_INF, F32)
    acc_ref[...] = jnp.zeros((CKVT_ROWS, cols), F32)
    scores_into(sa_ref, 0)

    def pair_body(i, _):
        scores_into(sb_ref, 2 * i + 1)
        consume(sa_ref, 2 * i, False)
        scores_into(sa_ref, 2 * i + 2)
        consume(sb_ref, 2 * i + 1, False)
        return 0

    lax.fori_loop(0, n_full // 2, pair_body, 0)
    odd = n_full % 2 == 1

    @pl.when(odd)
    def _():
        scores_into(sb_ref, n_full)
        consume(sa_ref, n_full - 1, False)
        consume(sb_ref, n_full, True)

    @pl.when(jnp.logical_not(odd))
    def _():
        consume(sa_ref, n_full, True)

    acc = acc_ref[...]
    o_t = acc[:KV_LORA] / acc[KV_LORA:KV_LORA + 1]
    for h in range(MLA_HEADS):
        olat_ref[0, h] = o_t[:, h * blk:(h + 1) * blk].T


def _prompt_attn_call(qa, ka, va, qcat, kcat, ckvt):
    nb, _, t, _ = qa.shape
    blk = ATTN_BLOCK
    q_spec = lambda n, w: pl.BlockSpec((1, n, blk, w), lambda b, i: (b, 0, i, 0))
    kv_spec = pl.BlockSpec((1, SB_PAIRS, t, LANES), lambda b, i: (b, 0, 0, 0))
    return pl.pallas_call(
        _prompt_attn_kernel,
        grid=(nb, t // blk),
        in_specs=[q_spec(SB_PAIRS, LANES), kv_spec, kv_spec, q_spec(MLA_HEADS, QCAT),
                  pl.BlockSpec((1, t, QCAT), lambda b, i: (b, 0, 0)),
                  pl.BlockSpec((1, CKVT_ROWS, t), lambda b, i: (b, 0, 0))],
        out_specs=(q_spec(SB_PAIRS, LANES), q_spec(MLA_HEADS, KV_LORA)),
        out_shape=(jax.ShapeDtypeStruct((nb, SB_PAIRS, t, LANES), F32),
                   jax.ShapeDtypeStruct((nb, MLA_HEADS, t, KV_LORA), F32)),
        scratch_shapes=[pltpu.VMEM((MLA_CHUNK, MLA_HEADS * blk), F32),
                        pltpu.VMEM((MLA_CHUNK, MLA_HEADS * blk), F32),
                        pltpu.VMEM((1, MLA_HEADS * blk), F32),
                        pltpu.VMEM((CKVT_ROWS, MLA_HEADS * blk), F32)],
        compiler_params=pltpu.CompilerParams(
            dimension_semantics=("parallel", "parallel"), vmem_limit_bytes=VMEM_LIMIT_BYTES),
        name="prompt_attn",
    )(qa, ka, va, qcat, kcat, ckvt)


def _sb_decode_kernel(pt_ref, q_ref, knew_hbm, vnew_hbm, kc_hbm, vc_hbm, o_ref,
                      kbuf, vbuf, kslow, vslow, sem, slow_sem, *, n_pages, n_new):
    step = pl.program_id(0)
    n_steps = pl.num_programs(0)
    grp = q_ref.shape[0]
    page = kc_hbm.shape[-1]
    first = SBD_PAGES * page
    ntok = first + page

    def first_copies(s, slot):
        out = []
        for g in range(grp):
            b = s * grp + g
            for hbm, new, buf, t in ((kc_hbm, knew_hbm, kbuf, 0), (vc_hbm, vnew_hbm, vbuf, 1)):
                for j in range(SBD_PAGES):
                    pid = pt_ref[b * n_pages + (n_pages - SBD_PAGES + j)]
                    out.append(pltpu.make_async_copy(
                        hbm.at[pid], buf.at[slot * grp + g, :, :, pl.ds(j * page, page)], sem.at[slot, t]))
                tile = pl.multiple_of((b * n_new) // page * page, page)
                out.append(pltpu.make_async_copy(
                    new.at[:, :, pl.ds(tile, page)], buf.at[slot * grp + g, :, :, pl.ds(first, page)],
                    sem.at[slot, t]))
        return out

    @pl.when(step == 0)
    def _():
        for c in first_copies(0, 0):
            c.start()

    slot = step % 2

    @pl.when(step + 1 < n_steps)
    def _():
        for c in first_copies(step + 1, 1 - slot):
            c.start()

    for c in first_copies(step, slot):
        c.wait()

    rows = SB_HEADS * SUBLANES
    r_iota = lax.broadcasted_iota(jnp.int32, (rows, ntok), 0)
    c_iota = lax.broadcasted_iota(jnp.int32, (rows, ntok), 1)
    real_row = (lax.broadcasted_iota(jnp.int32, (rows, 1), 0) % SUBLANES) < n_new
    tri = _tri(ntok)
    tri_slow = _tri(first)

    def scores(qs, buf, idx):
        return jnp.concatenate([_dot(qs[h], buf[idx, h].astype(BF16)) for h in range(SB_HEADS)], axis=0)

    def weighted(a, buf, idx):
        v_all = buf[idx].reshape(SB_WIDTH, -1).astype(BF16)
        full = _dot_nt(a, v_all)
        return jnp.concatenate(
            [full[h * SUBLANES:(h + 1) * SUBLANES, h * SB_HEAD_DIM:(h + 1) * SB_HEAD_DIM]
             for h in range(SB_HEADS)], axis=0)

    qss, zs, valids = [], [], []
    for g in range(grp):
        j_new = c_iota - (first + ((step * grp + g) * n_new) % page)
        valids.append(jnp.logical_or(c_iota < first, jnp.logical_and(j_new >= 0, j_new < r_iota % SUBLANES)))
        qss.append([q_ref[g, h].astype(BF16) for h in range(SB_HEADS)])
        zs.append(scores(qss[g], kbuf, slot * grp + g))
    sp = jnp.concatenate([jnp.where(valids[g], _softplus(zs[g]), 0.0) for g in range(grp)], axis=0)
    suf_all = _suffix_sums(sp, tri)
    newest = []
    for g in range(grp):
        suf = suf_all[g * rows:(g + 1) * rows]
        a = jnp.where(valids[g], jnp.exp(zs[g] - suf), 0.0).astype(BF16)
        newest.append((qss[g], suf[:, 0:1], weighted(a, vbuf, slot * grp + g)))

    for g in range(grp):
        b = step * grp + g
        qs, carry, acc = newest[g]

        def cond(st):
            c, carry, _ = st
            live = jnp.min(jnp.where(real_row, carry, SB_EXIT)) < SB_EXIT
            return jnp.logical_and(c >= 0, live)

        def body(st):
            c, carry, acc = st
            copies = []
            for hbm, buf, t in ((kc_hbm, kslow, 0), (vc_hbm, vslow, 1)):
                for j in range(SBD_PAGES):
                    pid = pt_ref[b * n_pages + c * SBD_PAGES + j]
                    copies.append(pltpu.make_async_copy(
                        hbm.at[pid], buf.at[0, :, :, pl.ds(j * page, page)], slow_sem.at[t]))
            for cp in copies:
                cp.start()
            for cp in copies:
                cp.wait()
            z = scores(qs, kslow, 0)
            suf = _suffix_sums(_softplus(z), tri_slow)
            a = jnp.exp(z - suf - carry).astype(BF16)
            return c - 1, carry + suf[:, 0:1], acc + weighted(a, vslow, 0)

        _, _, acc = lax.while_loop(cond, body, (n_pages // SBD_PAGES - 2, carry, acc))
        o_ref[g] = acc.reshape(SB_HEADS, SUBLANES, SB_HEAD_DIM)


def _sb_decode_call(pt_flat, q8, knew_t, vnew_t, cache_kt, cache_vt, n_pages, n_new):
    nbatch = q8.shape[0]
    page = cache_kt.shape[-1]
    grp = SBD_GROUP
    ntok = (SBD_PAGES + 1) * page
    any_spec = pl.BlockSpec(memory_space=pl.ANY)
    blk = pl.BlockSpec((grp, SB_HEADS, SUBLANES, SB_HEAD_DIM), lambda s, pt: (s, 0, 0, 0))
    kernel = functools.partial(_sb_decode_kernel, n_pages=n_pages, n_new=n_new)
    return pl.pallas_call(
        kernel,
        grid_spec=pltpu.PrefetchScalarGridSpec(
            num_scalar_prefetch=1,
            grid=(nbatch // grp,),
            in_specs=[blk, any_spec, any_spec, any_spec, any_spec],
            out_specs=blk,
            scratch_shapes=[
                pltpu.VMEM((2 * grp, SB_HEADS, SB_HEAD_DIM, ntok), F32),
                pltpu.VMEM((2 * grp, SB_HEADS, SB_HEAD_DIM, ntok), F32),
                pltpu.VMEM((1, SB_HEADS, SB_HEAD_DIM, SBD_PAGES * page), F32),
                pltpu.VMEM((1, SB_HEADS, SB_HEAD_DIM, SBD_PAGES * page), F32),
                pltpu.SemaphoreType.DMA((2, 2)),
                pltpu.SemaphoreType.DMA((2,)),
            ]),
        out_shape=jax.ShapeDtypeStruct((nbatch, SB_HEADS, SUBLANES, SB_HEAD_DIM), F32),
        compiler_params=pltpu.CompilerParams(
            dimension_semantics=("arbitrary",), vmem_limit_bytes=VMEM_LIMIT_BYTES),
        name="sb_decode",
    )(pt_flat, q8, knew_t, vnew_t, cache_kt, cache_vt)


def _mla_decode_kernel(pt_ref, q_ref, ckvnew_ref, kpenew_ref, ckv_hbm, kpe_hbm, o_ref,
                       ckvbuf, kpebuf, m_ref, l_ref, acc_ref, sem, *, n_pages):
    b = pl.program_id(0)
    c = pl.program_id(1)
    n_chunks = pl.num_programs(1)
    n = b * n_chunks + c
    total = pl.num_programs(0) * n_chunks
    page = ckv_hbm.shape[1]
    chunk_pages = ckvbuf.shape[1] // page

    def copies(i, slot):
        bb = i // n_chunks
        cc = i % n_chunks
        out = []
        for j in range(chunk_pages):
            pid = pt_ref[bb * n_pages + cc * chunk_pages + j]
            out.append(pltpu.make_async_copy(
                ckv_hbm.at[pid], ckvbuf.at[slot, pl.ds(j * page, page), :], sem.at[slot, 0]))
            out.append(pltpu.make_async_copy(
                kpe_hbm.at[pid], kpebuf.at[slot, :, pl.ds(j * page, page)], sem.at[slot, 1]))
        return out

    @pl.when(n == 0)
    def _():
        for cp in copies(0, 0):
            cp.start()

    slot = n % 2

    @pl.when(n + 1 < total)
    def _():
        for cp in copies(n + 1, 1 - slot):
            cp.start()

    q = q_ref[0]
    q_lat = q[:, :KV_LORA]
    q_pe = q[:, KV_LORA:KV_LORA + MLA_ROPE_DIM]

    def scores(ckv, kpe_t):
        return _dot_nt(q_lat, ckv) + _dot(q_pe, kpe_t)

    @pl.when(c == 0)
    def _():
        ckv = ckvnew_ref[0].astype(BF16)
        s = scores(ckv, kpenew_ref[0].astype(BF16))
        r_iota = lax.broadcasted_iota(jnp.int32, s.shape, 0)
        c_iota = lax.broadcasted_iota(jnp.int32, s.shape, 1)
        s = jnp.where(c_iota <= r_iota // MLA_HEADS, s, NEG_INF)
        m = jnp.max(s, axis=-1, keepdims=True)
        p = jnp.exp2(s - m)
        m_ref[...] = m
        l_ref[...] = jnp.sum(p, axis=-1, keepdims=True)
        acc_ref[...] = _dot(p.astype(BF16), ckv)

    for cp in copies(n, slot):
        cp.wait()

    ckv = ckvbuf[slot].astype(BF16)
    s = scores(ckv, kpebuf[slot].astype(BF16))
    m_old = m_ref[...]
    m_new = jnp.maximum(m_old, jnp.max(s, axis=-1, keepdims=True))
    p = jnp.exp2(s - m_new)
    alpha = jnp.exp2(m_old - m_new)
    l_new = alpha * l_ref[...] + jnp.sum(p, axis=-1, keepdims=True)
    acc_new = alpha * acc_ref[...] + _dot(p.astype(BF16), ckv)
    m_ref[...] = m_new
    l_ref[...] = l_new
    acc_ref[...] = acc_new

    @pl.when(c == n_chunks - 1)
    def _():
        o_ref[0] = acc_new / l_new


def _mla_decode_call(pt_flat, qcs, ckv_new, kpe_new_t, cache_ckv, cache_kpe_t, n_pages):
    nbatch, rows, _ = qcs.shape
    page = cache_ckv.shape[1]
    chunk_pages = min(MLAD_PAGES, n_pages)
    assert n_pages % chunk_pages == 0
    chunk = chunk_pages * page
    any_spec = pl.BlockSpec(memory_space=pl.ANY)
    per_b = lambda shape: pl.BlockSpec((1,) + shape, lambda b, c, pt: (b, 0, 0))
    kernel = functools.partial(_mla_decode_kernel, n_pages=n_pages)
    return pl.pallas_call(
        kernel,
        grid_spec=pltpu.PrefetchScalarGridSpec(
            num_scalar_prefetch=1,
            grid=(nbatch, n_pages // chunk_pages),
            in_specs=[per_b((rows, QCAT)), per_b((page, KV_LORA)), per_b((MLA_ROPE_DIM, page)),
                      any_spec, any_spec],
            out_specs=per_b((rows, KV_LORA)),
            scratch_shapes=[
                pltpu.VMEM((2, chunk, KV_LORA), F32),
                pltpu.VMEM((2, MLA_ROPE_DIM, chunk), F32),
                pltpu.VMEM((rows, 1), F32),
                pltpu.VMEM((rows, 1), F32),
                pltpu.VMEM((rows, KV_LORA), F32),
                pltpu.SemaphoreType.DMA((2, 2)),
            ]),
        out_shape=jax.ShapeDtypeStruct((nbatch, rows, KV_LORA), F32),
        compiler_params=pltpu.CompilerParams(
            dimension_semantics=("arbitrary", "arbitrary"), vmem_limit_bytes=VMEM_LIMIT_BYTES),
        name="mla_decode",
    )(pt_flat, qcs, ckv_new, kpe_new_t, cache_ckv, cache_kpe_t)


def _prep_ffn(g_pre, w_gate, w_up, w_down, g_post):
    d, d_ff = w_gate.shape
    assert d_ff % FFN_CHUNK == 0
    return (g_pre.reshape(1, d), w_gate.astype(BF16), w_up.astype(BF16), w_down.astype(BF16),
            g_post.reshape(1, d))


def _prep_proj(g_mix_pre, w_in, g_q_lora, w_q_up, w_uk, g_kv_lora):
    d = w_in.shape[0]
    half = MLA_ROPE_DIM // 2
    kpe_w = w_in[:, _IN_KPE:_IN_KPE + MLA_ROPE_DIM]
    pad = jnp.zeros((d, LANES - MLA_ROPE_DIM), w_in.dtype)
    w_in_ext = jnp.concatenate([
        w_in[:, :SB_WIDTH] * SB_SCALE, w_in[:, SB_WIDTH:_IN_KPE],
        kpe_w, pad, kpe_w[:, half:], kpe_w[:, :half], pad], axis=1).astype(BF16)

    wq = w_q_up.reshape(Q_LORA, MLA_HEADS, MLA_QK_DIM)
    nope = wq[:, :, :MLA_NOPE_DIM].reshape(Q_LORA, MLA_HEADS * MLA_NOPE_DIM)
    pe = wq[:, :, MLA_NOPE_DIM:]
    pe_swap = jnp.concatenate([pe[:, :, half:], pe[:, :, :half]], axis=-1)
    lane_pad = lambda a: jnp.pad(a, ((0, 0), (0, 0), (0, LANES - MLA_ROPE_DIM))).reshape(Q_LORA, MLA_HEADS * LANES)
    w_qup_ext = jnp.concatenate([nope, lane_pad(pe), lane_pad(pe_swap)], axis=1).astype(BF16)

    eye = jnp.eye(MLA_HEADS, dtype=w_uk.dtype)
    w_uk_bd = jnp.einsum('chn,hg->hngc', w_uk, eye).reshape(MLA_HEADS * MLA_NOPE_DIM, MLA_HEADS * KV_LORA)
    return (g_mix_pre.reshape(1, d), w_in_ext, g_q_lora.reshape(1, Q_LORA), w_qup_ext,
            _pair_blocks(w_uk_bd).astype(BF16), g_kv_lora.reshape(1, KV_LORA))


def _pair_blocks(w_bd):
    n = MLA_HEADS // 2
    r, c = w_bd.shape[0] // n, w_bd.shape[1] // n
    return jnp.stack([w_bd[p * r:(p + 1) * r, p * c:(p + 1) * c] for p in range(n)])


def _prep_out(g_sb_out, g_mla_out, w_uv, w_out, g_mix_post):
    eye = jnp.eye(MLA_HEADS, dtype=w_uv.dtype)
    w_uv_bd = jnp.einsum('chv,hg->hcgv', w_uv, eye).reshape(MLA_HEADS * KV_LORA, MLA_WIDTH)
    return (g_sb_out.reshape(1, SB_WIDTH), g_mla_out.reshape(1, MLA_WIDTH), _pair_blocks(w_uv_bd).astype(BF16),
            w_out.astype(BF16), g_mix_post.reshape(1, -1))


def _rope_tables(pos):
    half = MLA_ROPE_DIM // 2
    inv_freq = ROPE_THETA ** (-jnp.arange(half, dtype=F32) / half)
    ang = pos.astype(F32)[:, None] * inv_freq[None, :]
    cos, sin = jnp.cos(ang), jnp.sin(ang)
    pad = jnp.zeros((pos.shape[0], LANES - MLA_ROPE_DIM), F32)
    return (jnp.concatenate([cos, cos, pad], axis=1), jnp.concatenate([-sin, sin, pad], axis=1))


def _token_tile(t):
    for tm in (256, 128):
        if t % tm == 0:
            return tm
    raise ValueError(f"token count {t} is not a multiple of 128")


def kernel(x_prompt, x_sample, cache_sb_k, cache_sb_v, cache_mla_ckv, cache_mla_kpe, page_table,
           g_ffn1_pre, w_ffn1_gate, w_ffn1_up, w_ffn1_down, g_ffn1_post,
           g_mix_pre, w_in, g_q_lora, w_q_up, g_kv_lora, w_uk, w_uv, g_sb_out, g_mla_out, w_out, g_mix_post,
           g_ffn2_pre, w_ffn2_gate, w_ffn2_up, w_ffn2_down, g_ffn2_post):
    batch, seq, d = x_prompt.shape
    dec_batch, dec_seq, _ = x_sample.shape
    page = cache_sb_k.shape[1]
    n_pages = page_table.shape[1]
    past_len = n_pages * page
    assert seq % MLA_CHUNK == 0 and MLA_CHUNK % ATTN_BLOCK == 0 and dec_seq <= SUBLANES
    assert n_pages % SBD_PAGES == 0 and n_pages >= 2 * SBD_PAGES
    assert dec_batch % SBD_GROUP == 0 and page % dec_seq == 0 and (dec_batch * dec_seq) % page == 0

    ffn1 = _prep_ffn(g_ffn1_pre, w_ffn1_gate, w_ffn1_up, w_ffn1_down, g_ffn1_post)
    ffn2 = _prep_ffn(g_ffn2_pre, w_ffn2_gate, w_ffn2_up, w_ffn2_down, g_ffn2_post)
    proj_w = _prep_proj(g_mix_pre, w_in, g_q_lora, w_q_up, w_uk, g_kv_lora)
    out_w = _prep_out(g_sb_out, g_mla_out, w_uv, w_out, g_mix_post)

    cos_p, sin_p = _rope_tables(jnp.arange(seq))
    tm_p = _token_tile(seq)
    (x1_p, sbk_p, sbv_p, ckv_p, kpe_p, qa_p, ka_p, va_p, qcat_p, kcat_p, ckvt_p) = _ffn_proj_call(
        x_prompt, cos_p, sin_p, ffn1, proj_w, tm_p)
    osb_p, olat_p = _prompt_attn_call(qa_p, ka_p, va_p, qcat_p, kcat_p, ckvt_p)
    y_p = _out_ffn_call(osb_p, olat_p, x1_p, out_w, ffn2, tm_p)

    n_tok = dec_batch * dec_seq
    cos_s, sin_s = _rope_tables(past_len + (jnp.arange(n_tok) % dec_seq))
    tm_s = _token_tile(n_tok)
    (x1_s, sbk_s, sbv_s, ckv_s, kpe_s, qa_s, _, _, qcat_s, _, _) = _ffn_proj_call(
        x_sample.reshape(1, n_tok, d), cos_s, sin_s, ffn1, proj_w, tm_s)

    pt_flat = page_table.reshape(-1)
    pad_tok = page - dec_seq
    q8 = qa_s[0].transpose(1, 0, 2).reshape(dec_batch, dec_seq, SB_HEADS, SB_HEAD_DIM).transpose(0, 2, 1, 3)
    q8 = jnp.pad(q8, ((0, 0), (0, 0), (0, SUBLANES - dec_seq), (0, 0))).astype(F32)
    head_dim_tok = lambda a: a.reshape(SB_HEADS, SB_HEAD_DIM, n_tok)
    o8 = _sb_decode_call(pt_flat, q8, head_dim_tok(sbk_s), head_dim_tok(sbv_s),
                         cache_sb_k.transpose(0, 2, 3, 1), cache_sb_v.transpose(0, 2, 3, 1), n_pages, dec_seq)
    osb_s = o8[:, :, :dec_seq, :].transpose(0, 2, 1, 3).reshape(n_tok, SB_PAIRS, LANES).transpose(1, 0, 2)[None]

    qcs = qcat_s[0].reshape(MLA_HEADS, dec_batch, dec_seq, QCAT).transpose(1, 2, 0, 3)
    qcs = qcs.reshape(dec_batch, dec_seq * MLA_HEADS, QCAT)
    ckv_new = jnp.pad(ckv_s.reshape(dec_batch, dec_seq, KV_LORA), ((0, 0), (0, pad_tok), (0, 0)))
    kpe_new_t = jnp.pad(kpe_s.reshape(dec_batch, dec_seq, MLA_ROPE_DIM).transpose(0, 2, 1),
                        ((0, 0), (0, 0), (0, pad_tok)))
    olat = _mla_decode_call(pt_flat, qcs, ckv_new, kpe_new_t, cache_mla_ckv,
                            cache_mla_kpe.transpose(0, 2, 1), n_pages)
    olat_s = olat.reshape(dec_batch, dec_seq, MLA_HEADS, KV_LORA).transpose(2, 0, 1, 3)
    olat_s = olat_s.reshape(1, MLA_HEADS, n_tok, KV_LORA)

    y_s = _out_ffn_call(osb_s, olat_s, x1_s, out_w, ffn2, tm_s)

    tok_head_dim = lambda a, lead: a.reshape(lead + (SB_HEADS, SB_HEAD_DIM, -1)).transpose(
        tuple(range(len(lead))) + (len(lead) + 2, len(lead), len(lead) + 1))
    return (y_p, y_s.reshape(dec_batch, dec_seq, d),
            tok_head_dim(sbk_p, (batch,)), tok_head_dim(sbv_p, (batch,)),
            ckv_p, kpe_p,
            tok_head_dim(sbk_s[0], ()).reshape(dec_batch, dec_seq, SB_HEADS, SB_HEAD_DIM),
            tok_head_dim(sbv_s[0], ()).reshape(dec_batch, dec_seq, SB_HEADS, SB_HEAD_DIM),
            ckv_s.reshape(dec_batch, dec_seq, KV_LORA), kpe_s.reshape(dec_batch, dec_seq, MLA_ROPE_DIM))
```

```python
import functools

import jax
import jax.numpy as jnp
from jax import lax
from jax.experimental import pallas as pl
from jax.experimental.pallas import tpu as pltpu

F32 = jnp.float32
BF16 = jnp.bfloat16

SB_HEADS = 8
SB_HEAD_DIM = 64
SB_WIDTH = SB_HEADS * SB_HEAD_DIM
SB_PAIRS = SB_HEADS // 2
MLA_HEADS = 8
MLA_NOPE_DIM = 64
MLA_ROPE_DIM = 32
MLA_V_DIM = 64
MLA_QK_DIM = MLA_NOPE_DIM + MLA_ROPE_DIM
MLA_WIDTH = MLA_HEADS * MLA_V_DIM
Q_LORA = 256
KV_LORA = 128
FFN_RESIDUAL_WEIGHT = 0.5
ROPE_THETA = 10000.0
NORM_EPS = 1e-6
SB_SCALE = SB_HEAD_DIM ** -0.5
MLA_SCALE = MLA_QK_DIM ** -0.5
LOG2_E = 1.4426950408889634
MLA_EXP2_SCALE = MLA_SCALE * LOG2_E
NEG_INF = -1e30

LANES = 128
SUBLANES = 8
MXU_DIM = 256
VMEM_LIMIT_BYTES = 56 * 1024 * 1024

FFN_CHUNK = MXU_DIM
OUT_FFN_TILE = 2 * MXU_DIM
ATTN_BLOCK = MXU_DIM
MLA_CHUNK = 2 * MXU_DIM
CKVT_ROWS = KV_LORA + 16
QCAT = 2 * LANES
SB_EXIT = 104.0

SBD_GROUP = 4
SBD_PAGES = 2
MLAD_PAGES = 64


def _dot(a, b):
    return jnp.dot(a, b, preferred_element_type=F32)


def _dot_nt(a, b):
    return lax.dot_general(a, b, (((1,), (1,)), ((), ())), preferred_element_type=F32)


def _rms(x, g):
    ms = jnp.mean(x * x, axis=-1, keepdims=True)
    return x * lax.rsqrt(ms + NORM_EPS) * g


def _softplus(z):
    return jnp.maximum(z, 0.0) + jnp.log(1.0 + jnp.exp(-jnp.abs(z)))


def _split_bf16(x):
    hi = x.astype(BF16)
    lo = (x - hi.astype(F32)).astype(BF16)
    return hi, lo


def _suffix_sums(x, tri):
    hi, lo = _split_bf16(x)
    return _dot(hi, tri) + _dot(lo, tri)


def _tri(n):
    row = lax.broadcasted_iota(jnp.int32, (n, n), 0)
    col = lax.broadcasted_iota(jnp.int32, (n, n), 1)
    return (row >= col).astype(BF16)


def _const_spec(shape):
    zeros = (0,) * len(shape)
    return pl.BlockSpec(shape, lambda *_: zeros, pipeline_mode=pl.Buffered(1))


def _half_step_ffn(x, ffn_refs, act_ref):
    g_pre_ref, wg_ref, wu_ref, wd_ref, g_post_ref = ffn_refs
    h = _rms(x, g_pre_ref[...]).astype(BF16)
    for c in range(wg_ref.shape[1] // FFN_CHUNK):
        cols = slice(c * FFN_CHUNK, (c + 1) * FFN_CHUNK)
        g = _dot(h, wg_ref[:, cols])
        u = _dot(h, wu_ref[:, cols])
        act_ref[:, cols] = ((g * jax.nn.sigmoid(g)) * u).astype(BF16)
    y = _dot(act_ref[...], wd_ref[...])
    return x + FFN_RESIDUAL_WEIGHT * _rms(y, g_post_ref[...])


_IN_Q = 0
_IN_K = SB_WIDTH
_IN_V = 2 * SB_WIDTH
_IN_QD = 3 * SB_WIDTH
_IN_KVD = _IN_QD + Q_LORA
_IN_KPE = _IN_KVD + KV_LORA
_IN_KPE_SWAP = _IN_KPE + LANES
_IN_COLS_EXT = _IN_KPE_SWAP + LANES
_QUP_NOPE = 0
_QUP_PE = MLA_HEADS * MLA_NOPE_DIM
_QUP_PE_SWAP = _QUP_PE + MLA_HEADS * LANES
_QUP_COLS_EXT = _QUP_PE_SWAP + MLA_HEADS * LANES


def _ffn_proj_kernel(x_ref, cos_ref, sin_ref, g_pre_ref, wg_ref, wu_ref, wd_ref, g_post_ref,
                     g_mix_ref, w_in_ref, g_q_ref, w_qup_ref, w_uk_ref, g_kv_ref,
                     x1_ref, sbk_ref, sbv_ref, ckv_ref, kpe_ref,
                     qa_ref, ka_ref, va_ref, qcat_ref, kcat_ref, ckvt_ref, act_ref):
    x1 = _half_step_ffn(x_ref[0], (g_pre_ref, wg_ref, wu_ref, wd_ref, g_post_ref), act_ref)
    x1_ref[0] = x1

    hm = _rms(x1, g_mix_ref[...]).astype(BF16)
    proj = _dot(hm, w_in_ref[...])
    sbk = proj[:, _IN_K:_IN_K + SB_WIDTH]
    sbv = proj[:, _IN_V:_IN_V + SB_WIDTH]
    sbk_ref[0] = sbk.T
    sbv_ref[0] = sbv.T
    for p in range(SB_PAIRS):
        lo, hi = p * LANES, (p + 1) * LANES
        qa_ref[0, p] = proj[:, _IN_Q + lo:_IN_Q + hi].astype(BF16)
        ka_ref[0, p] = sbk[:, lo:hi].astype(BF16)
        va_ref[0, p] = sbv[:, lo:hi].astype(BF16)

    cos = cos_ref[...]
    sin = sin_ref[...]
    qn = _rms(proj[:, _IN_QD:_IN_QD + Q_LORA], g_q_ref[...]).astype(BF16)
    qup = _dot(qn, w_qup_ref[...])
    qnope = qup[:, _QUP_NOPE:_QUP_PE].astype(BF16)
    qlat = jnp.concatenate([_dot(qnope[:, p * LANES:(p + 1) * LANES], w_uk_ref[p])
                            for p in range(MLA_HEADS // 2)], axis=-1)
    for h in range(MLA_HEADS):
        lo, hi = h * LANES, (h + 1) * LANES
        pe = (qup[:, _QUP_PE + lo:_QUP_PE + hi] * cos
              + qup[:, _QUP_PE_SWAP + lo:_QUP_PE_SWAP + hi] * sin)
        qcat_ref[0, h, :, :LANES] = (qlat[:, lo:hi] * MLA_EXP2_SCALE).astype(BF16)
        qcat_ref[0, h, :, LANES:] = (pe * MLA_EXP2_SCALE).astype(BF16)

    ckv = _rms(proj[:, _IN_KVD:_IN_KVD + KV_LORA], g_kv_ref[...])
    kpe = (proj[:, _IN_KPE:_IN_KPE + LANES] * cos
           + proj[:, _IN_KPE_SWAP:_IN_KPE_SWAP + LANES] * sin)
    ckv_ref[0] = ckv
    kpe_ref[0] = kpe[:, :MLA_ROPE_DIM]
    kcat_ref[0, :, :LANES] = ckv.astype(BF16)
    kcat_ref[0, :, LANES:] = kpe.astype(BF16)
    ckvt_ref[0, :KV_LORA, :] = ckv.T.astype(BF16)
    ckvt_ref[0, KV_LORA:, :] = jnp.ones((CKVT_ROWS - KV_LORA, ckv.shape[0]), BF16)


def _ffn_proj_call(x, cos_t, sin_t, ffn_w, proj_w, tm):
    nb, t, d = x.shape
    weights = tuple(ffn_w) + tuple(proj_w)
    d_ff = ffn_w[1].shape[1]
    tok = lambda shape: pl.BlockSpec((1, tm) + shape, lambda b, i: (b, i) + (0,) * len(shape))
    heads = lambda n, w: pl.BlockSpec((1, n, tm, w), lambda b, i: (b, 0, i, 0))
    tab = pl.BlockSpec((tm, LANES), lambda b, i: (i, 0))
    in_specs = [tok((d,)), tab, tab] + [_const_spec(w.shape) for w in weights]
    out_shape = (
        jax.ShapeDtypeStruct((nb, t, d), F32),
        jax.ShapeDtypeStruct((nb, SB_WIDTH, t), F32),
        jax.ShapeDtypeStruct((nb, SB_WIDTH, t), F32),
        jax.ShapeDtypeStruct((nb, t, KV_LORA), F32),
        jax.ShapeDtypeStruct((nb, t, MLA_ROPE_DIM), F32),
        jax.ShapeDtypeStruct((nb, SB_PAIRS, t, LANES), BF16),
        jax.ShapeDtypeStruct((nb, SB_PAIRS, t, LANES), BF16),
        jax.ShapeDtypeStruct((nb, SB_PAIRS, t, LANES), BF16),
        jax.ShapeDtypeStruct((nb, MLA_HEADS, t, QCAT), BF16),
        jax.ShapeDtypeStruct((nb, t, QCAT), BF16),
        jax.ShapeDtypeStruct((nb, CKVT_ROWS, t), BF16),
    )
    tok_minor = lambda n: pl.BlockSpec((1, n, tm), lambda b, i: (b, 0, i))
    out_specs = (tok((d,)), tok_minor(SB_WIDTH), tok_minor(SB_WIDTH), tok((KV_LORA,)), tok((MLA_ROPE_DIM,)),
                 heads(SB_PAIRS, LANES), heads(SB_PAIRS, LANES), heads(SB_PAIRS, LANES),
                 heads(MLA_HEADS, QCAT), tok((QCAT,)), tok_minor(CKVT_ROWS))
    return pl.pallas_call(
        _ffn_proj_kernel,
        grid=(nb, t // tm),
        in_specs=in_specs,
        out_specs=out_specs,
        out_shape=out_shape,
        scratch_shapes=[pltpu.VMEM((tm, d_ff), BF16)],
        compiler_params=pltpu.CompilerParams(
            dimension_semantics=("parallel", "parallel"), vmem_limit_bytes=VMEM_LIMIT_BYTES),
        name="ffn_proj",
    )(x, cos_t, sin_t, *weights)


def _out_ffn_kernel(osb_ref, olat_ref, x1_ref, g_sb_ref, g_mla_ref, w_uv_ref, w_out_ref, g_mixpost_ref,
                    g_pre_ref, wg_ref, wu_ref, wd_ref, g_post_ref, y_ref, act_ref):
    osb = jnp.concatenate([osb_ref[0, p] for p in range(SB_PAIRS)], axis=-1)
    olat = jnp.concatenate([olat_ref[0, h] for h in range(MLA_HEADS)], axis=-1)
    olat = olat.astype(BF16)
    omla = jnp.concatenate([_dot(olat[:, p * 2 * KV_LORA:(p + 1) * 2 * KV_LORA], w_uv_ref[p])
                            for p in range(MLA_HEADS // 2)], axis=-1)
    merged = jnp.concatenate([_rms(osb, g_sb_ref[...]), _rms(omla, g_mla_ref[...])], axis=-1)
    o = _dot(merged.astype(BF16), w_out_ref[...])
    x2 = x1_ref[0] + _rms(o, g_mixpost_ref[...])
    y_ref[0] = _half_step_ffn(x2, (g_pre_ref, wg_ref, wu_ref, wd_ref, g_post_ref), act_ref)


def _out_ffn_call(osb, olat, x1, out_w, ffn_w, tm):
    nb, t, d = x1.shape
    d_ff = ffn_w[1].shape[1]
    heads = lambda n: pl.BlockSpec((1, n, tm, LANES), lambda b, i: (b, 0, i, 0))
    tok = pl.BlockSpec((1, tm, d), lambda b, i: (b, i, 0))
    weights = tuple(out_w) + tuple(ffn_w)
    return pl.pallas_call(
        _out_ffn_kernel,
        grid=(nb, t // tm),
        in_specs=[heads(SB_PAIRS), heads(MLA_HEADS), tok] + [_const_spec(w.shape) for w in weights],
        out_specs=tok,
        out_shape=jax.ShapeDtypeStruct((nb, t, d), F32),
        scratch_shapes=[pltpu.VMEM((tm, d_ff), BF16)],
        compiler_params=pltpu.CompilerParams(
            dimension_semantics=("parallel", "parallel"), vmem_limit_bytes=VMEM_LIMIT_BYTES),
        name="out_ffn",
    )(osb, olat, x1, *weights)


def _prompt_attn_kernel(qa_ref, ka_ref, va_ref, qcat_ref, kcat_ref, ckvt_ref, osb_ref, olat_ref,
                        sa_ref, sb_ref, m_ref, acc_ref):
    qi = pl.program_id(1)
    blk = ATTN_BLOCK
    tri = _tri(blk)
    rows_sb = lax.broadcasted_iota(jnp.int32, (SB_HEADS * blk, blk), 0)
    strict = lax.broadcasted_iota(jnp.int32, (SB_HEADS * blk, blk), 1) < rows_sb % blk
    even_q = jnp.logical_xor(lax.broadcasted_iota(jnp.int32, (2 * blk, LANES), 1) >= SB_HEAD_DIM,
                             lax.broadcasted_iota(jnp.int32, (2 * blk, LANES), 0) < blk)
    diag = pl.multiple_of(qi * blk, blk)

    def pair_operands(p, off):
        k = ka_ref[0, p, pl.ds(off, blk), :]
        v = va_ref[0, p, pl.ds(off, blk), :]
        v2 = jnp.concatenate([v, v], axis=0)
        return k, jnp.where(even_q, v2, jnp.zeros_like(v2))

    q2s = []
    for p in range(SB_PAIRS):
        qp = qa_ref[0, p]
        qp2 = jnp.concatenate([qp, qp], axis=0)
        q2s.append(jnp.where(even_q, qp2, jnp.zeros_like(qp2)))

    def sb_step(off, carry, accs, masked):
        operands = [pair_operands(p, off) for p in range(SB_PAIRS)]
        z = jnp.concatenate([_dot_nt(q2s[p], operands[p][0]) for p in range(SB_PAIRS)], axis=0)
        sp = _softplus(z)
        if masked:
            sp = jnp.where(strict, sp, 0.0)
        suf = _dot(sp.astype(BF16), tri)
        a = jnp.exp(z - suf - carry)
        if masked:
            a = jnp.where(strict, a, 0.0)
        a = a.astype(BF16)
        out = []
        for p in range(SB_PAIRS):
            a_even = a[2 * p * blk:(2 * p + 1) * blk]
            a_odd = a[(2 * p + 1) * blk:(2 * p + 2) * blk]
            out.append(accs[p] + _dot(jnp.concatenate([a_even, a_odd], axis=1), operands[p][1]))
        return carry + suf[:, 0:1], tuple(out)

    a0 = tuple(jnp.zeros((blk, LANES), F32) for _ in range(SB_PAIRS))
    carry, accs = sb_step(diag, jnp.zeros((SB_HEADS * blk, 1), F32), a0, True)

    def sb_cond(st):
        j, carry, _ = st
        return jnp.logical_and(j >= 0, jnp.min(carry) < SB_EXIT)

    def sb_body(st):
        j, carry, accs = st
        carry, accs = sb_step(pl.multiple_of(j * blk, blk), carry, accs, False)
        return j - 1, carry, accs

    _, _, accs = lax.while_loop(sb_cond, sb_body, (qi - 1, carry, accs))
    for p in range(SB_PAIRS):
        osb_ref[0, p] = accs[p]

    ch = MLA_CHUNK

    cols = MLA_HEADS * blk
    qc = qcat_ref[0].reshape(cols, QCAT)

    def scores_into(dst_ref, j):
        off = pl.multiple_of(j * ch, ch)
        dst_ref[...] = _dot_nt(kcat_ref[0, pl.ds(off, ch), :], qc)

    def consume(src_ref, j, masked):
        off = pl.multiple_of(j * ch, ch)
        s = src_ref[...]
        if masked:
            k_pos = off + lax.broadcasted_iota(jnp.int32, (ch, cols), 0)
            s = jnp.where(k_pos <= diag + lax.broadcasted_iota(jnp.int32, (ch, cols), 1) % blk, s, NEG_INF)
        m = m_ref[...]
        m_new = jnp.maximum(m, jnp.max(s, axis=0, keepdims=True))
        p = jnp.exp2(s - m_new)
        acc_ref[...] = jnp.exp2(m - m_new) * acc_ref[...] + _dot(ckvt_ref[0, :, pl.ds(off, ch)], p.astype(BF16))
        m_ref[...] = m_new

    n_full = diag // ch
    m_ref[...] = jnp.full((1, cols), NEG_INF, F32)
    acc_ref[...] = jnp.zeros((CKVT_ROWS, cols), F32)
    scores_into(sa_ref, 0)

    def pair_body(i, _):
        scores_into(sb_ref, 2 * i + 1)
        consume(sa_ref, 2 * i, False)
        scores_into(sa_ref, 2 * i + 2)
        consume(sb_ref, 2 * i + 1, False)
        return 0

    lax.fori_loop(0, n_full // 2, pair_body, 0)
    odd = n_full % 2 == 1

    @pl.when(odd)
    def _():
        scores_into(sb_ref, n_full)
        consume(sa_ref, n_full - 1, False)
        consume(sb_ref, n_full, True)

    @pl.when(jnp.logical_not(odd))
    def _():
        consume(sa_ref, n_full, True)

    acc = acc_ref[...]
    o_t = acc[:KV_LORA] / acc[KV_LORA:KV_LORA + 1]
    for h in range(MLA_HEADS):
        olat_ref[0, h] = o_t[:, h * blk:(h + 1) * blk].T


def _prompt_attn_call(qa, ka, va, qcat, kcat, ckvt):
    nb, _, t, _ = qa.shape
    blk = ATTN_BLOCK
    q_spec = lambda n, w: pl.BlockSpec((1, n, blk, w), lambda b, i: (b, 0, i, 0))
    kv_spec = pl.BlockSpec((1, SB_PAIRS, t, LANES), lambda b, i: (b, 0, 0, 0))
    return pl.pallas_call(
        _prompt_attn_kernel,
        grid=(nb, t // blk),
        in_specs=[q_spec(SB_PAIRS, LANES), kv_spec, kv_spec, q_spec(MLA_HEADS, QCAT),
                  pl.BlockSpec((1, t, QCAT), lambda b, i: (b, 0, 0)),
                  pl.BlockSpec((1, CKVT_ROWS, t), lambda b, i: (b, 0, 0))],
        out_specs=(q_spec(SB_PAIRS, LANES), q_spec(MLA_HEADS, KV_LORA)),
        out_shape=(jax.ShapeDtypeStruct((nb, SB_PAIRS, t, LANES), F32),
                   jax.ShapeDtypeStruct((nb, MLA_HEADS, t, KV_LORA), F32)),
        scratch_shapes=[pltpu.VMEM((MLA_CHUNK, MLA_HEADS * blk), F32),
                        pltpu.VMEM((MLA_CHUNK, MLA_HEADS * blk), F32),
                        pltpu.VMEM((1, MLA_HEADS * blk), F32),
                        pltpu.VMEM((CKVT_ROWS, MLA_HEADS * blk), F32)],
        compiler_params=pltpu.CompilerParams(
            dimension_semantics=("parallel", "parallel"), vmem_limit_bytes=VMEM_LIMIT_BYTES),
        name="prompt_attn",
    )(qa, ka, va, qcat, kcat, ckvt)


def _sb_decode_kernel(pt_ref, q_ref, knew_hbm, vnew_hbm, kc_hbm, vc_hbm, o_ref,
                      kbuf, vbuf, kslow, vslow, sem, slow_sem, *, n_pages, n_new):
    step = pl.program_id(0)
    n_steps = pl.num_programs(0)
    grp = q_ref.shape[0]
    page = kc_hbm.shape[-1]
    first = SBD_PAGES * page
    ntok = first + page

    def first_copies(s, slot):
        out = []
        for g in range(grp):
            b = s * grp + g
            for hbm, new, buf, t in ((kc_hbm, knew_hbm, kbuf, 0), (vc_hbm, vnew_hbm, vbuf, 1)):
                for j in range(SBD_PAGES):
                    pid = pt_ref[b * n_pages + (n_pages - SBD_PAGES + j)]
                    out.append(pltpu.make_async_copy(
                        hbm.at[pid], buf.at[slot * grp + g, :, :, pl.ds(j * page, page)], sem.at[slot, t]))
                tile = pl.multiple_of((b * n_new) // page * page, page)
                out.append(pltpu.make_async_copy(
                    new.at[:, :, pl.ds(tile, page)], buf.at[slot * grp + g, :, :, pl.ds(first, page)],
                    sem.at[slot, t]))
        return out

    @pl.when(step == 0)
    def _():
        for c in first_copies(0, 0):
            c.start()

    slot = step % 2

    @pl.when(step + 1 < n_steps)
    def _():
        for c in first_copies(step + 1, 1 - slot):
            c.start()

    for c in first_copies(step, slot):
        c.wait()

    rows = SB_HEADS * SUBLANES
    r_iota = lax.broadcasted_iota(jnp.int32, (rows, ntok), 0)
    c_iota = lax.broadcasted_iota(jnp.int32, (rows, ntok), 1)
    real_row = (lax.broadcasted_iota(jnp.int32, (rows, 1), 0) % SUBLANES) < n_new
    tri = _tri(ntok)
    tri_slow = _tri(first)

    def scores(qs, buf, idx):
        return jnp.concatenate([_dot(qs[h], buf[idx, h].astype(BF16)) for h in range(SB_HEADS)], axis=0)

    def weighted(a, buf, idx):
        v_all = buf[idx].reshape(SB_WIDTH, -1).astype(BF16)
        full = _dot_nt(a, v_all)
        return jnp.concatenate(
            [full[h * SUBLANES:(h + 1) * SUBLANES, h * SB_HEAD_DIM:(h + 1) * SB_HEAD_DIM]
             for h in range(SB_HEADS)], axis=0)

    qss, zs, valids = [], [], []
    for g in range(grp):
        j_new = c_iota - (first + ((step * grp + g) * n_new) % page)
        valids.append(jnp.logical_or(c_iota < first, jnp.logical_and(j_new >= 0, j_new < r_iota % SUBLANES)))
        qss.append([q_ref[g, h].astype(BF16) for h in range(SB_HEADS)])
        zs.append(scores(qss[g], kbuf, slot * grp + g))
    sp = jnp.concatenate([jnp.where(valids[g], _softplus(zs[g]), 0.0) for g in range(grp)], axis=0)
    suf_all = _suffix_sums(sp, tri)
    newest = []
    for g in range(grp):
        suf = suf_all[g * rows:(g + 1) * rows]
        a = jnp.where(valids[g], jnp.exp(zs[g] - suf), 0.0).astype(BF16)
        newest.append((qss[g], suf[:, 0:1], weighted(a, vbuf, slot * grp + g)))

    for g in range(grp):
        b = step * grp + g
        qs, carry, acc = newest[g]

        def cond(st):
            c, carry, _ = st
            live = jnp.min(jnp.where(real_row, carry, SB_EXIT)) < SB_EXIT
            return jnp.logical_and(c >= 0, live)

        def body(st):
            c, carry, acc = st
            copies = []
            for hbm, buf, t in ((kc_hbm, kslow, 0), (vc_hbm, vslow, 1)):
                for j in range(SBD_PAGES):
                    pid = pt_ref[b * n_pages + c * SBD_PAGES + j]
                    copies.append(pltpu.make_async_copy(
                        hbm.at[pid], buf.at[0, :, :, pl.ds(j * page, page)], slow_sem.at[t]))
            for cp in copies:
                cp.start()
            for cp in copies:
                cp.wait()
            z = scores(qs, kslow, 0)
            suf = _suffix_sums(_softplus(z), tri_slow)
            a = jnp.exp(z - suf - carry).astype(BF16)
            return c - 1, carry + suf[:, 0:1], acc + weighted(a, vslow, 0)

        _, _, acc = lax.while_loop(cond, body, (n_pages // SBD_PAGES - 2, carry, acc))
        o_ref[g] = acc.reshape(SB_HEADS, SUBLANES, SB_HEAD_DIM)


def _sb_decode_call(pt_flat, q8, knew_t, vnew_t, cache_kt, cache_vt, n_pages, n_new):
    nbatch = q8.shape[0]
    page = cache_kt.shape[-1]
    grp = SBD_GROUP
    ntok = (SBD_PAGES + 1) * page
    any_spec = pl.BlockSpec(memory_space=pl.ANY)
    blk = pl.BlockSpec((grp, SB_HEADS, SUBLANES, SB_HEAD_DIM), lambda s, pt: (s, 0, 0, 0))
    kernel = functools.partial(_sb_decode_kernel, n_pages=n_pages, n_new=n_new)
    return pl.pallas_call(
        kernel,
        grid_spec=pltpu.PrefetchScalarGridSpec(
            num_scalar_prefetch=1,
            grid=(nbatch // grp,),
            in_specs=[blk, any_spec, any_spec, any_spec, any_spec],
            out_specs=blk,
            scratch_shapes=[
                pltpu.VMEM((2 * grp, SB_HEADS, SB_HEAD_DIM, ntok), F32),
                pltpu.VMEM((2 * grp, SB_HEADS, SB_HEAD_DIM, ntok), F32),
                pltpu.VMEM((1, SB_HEADS, SB_HEAD_DIM, SBD_PAGES * page), F32),
                pltpu.VMEM((1, SB_HEADS, SB_HEAD_DIM, SBD_PAGES * page), F32),
                pltpu.SemaphoreType.DMA((2, 2)),
                pltpu.SemaphoreType.DMA((2,)),
            ]),
        out_shape=jax.ShapeDtypeStruct((nbatch, SB_HEADS, SUBLANES, SB_HEAD_DIM), F32),
        compiler_params=pltpu.CompilerParams(
            dimension_semantics=("arbitrary",), vmem_limit_bytes=VMEM_LIMIT_BYTES),
        name="sb_decode",
    )(pt_flat, q8, knew_t, vnew_t, cache_kt, cache_vt)


def _mla_decode_kernel(pt_ref, q_ref, ckvnew_ref, kpenew_ref, ckv_hbm, kpe_hbm, o_ref,
                       ckvbuf, kpebuf, m_ref, l_ref, acc_ref, sem, *, n_pages):
    b = pl.program_id(0)
    c = pl.program_id(1)
    n_chunks = pl.num_programs(1)
    n = b * n_chunks + c
    total = pl.num_programs(0) * n_chunks
    page = ckv_hbm.shape[1]
    chunk_pages = ckvbuf.shape[1] // page

    def copies(i, slot):
        bb = i // n_chunks
        cc = i % n_chunks
        out = []
        for j in range(chunk_pages):
            pid = pt_ref[bb * n_pages + cc * chunk_pages + j]
            out.append(pltpu.make_async_copy(
                ckv_hbm.at[pid], ckvbuf.at[slot, pl.ds(j * page, page), :], sem.at[slot, 0]))
            out.append(pltpu.make_async_copy(
                kpe_hbm.at[pid], kpebuf.at[slot, :, pl.ds(j * page, page)], sem.at[slot, 1]))
        return out

    @pl.when(n == 0)
    def _():
        for cp in copies(0, 0):
            cp.start()

    slot = n % 2

    @pl.when(n + 1 < total)
    def _():
        for cp in copies(n + 1, 1 - slot):
            cp.start()

    q = q_ref[0]
    q_lat = q[:, :KV_LORA]
    q_pe = q[:, KV_LORA:KV_LORA + MLA_ROPE_DIM]

    def scores(ckv, kpe_t):
        return _dot_nt(q_lat, ckv) + _dot(q_pe, kpe_t)

    @pl.when(c == 0)
    def _():
        ckv = ckvnew_ref[0].astype(BF16)
        s = scores(ckv, kpenew_ref[0].astype(BF16))
        r_iota = lax.broadcasted_iota(jnp.int32, s.shape, 0)
        c_iota = lax.broadcasted_iota(jnp.int32, s.shape, 1)
        s = jnp.where(c_iota <= r_iota // MLA_HEADS, s, NEG_INF)
        m = jnp.max(s, axis=-1, keepdims=True)
        p = jnp.exp2(s - m)
        m_ref[...] = m
        l_ref[...] = jnp.sum(p, axis=-1, keepdims=True)
        acc_ref[...] = _dot(p.astype(BF16), ckv)

    for cp in copies(n, slot):
        cp.wait()

    ckv = ckvbuf[slot].astype(BF16)
    s = scores(ckv, kpebuf[slot].astype(BF16))
    m_old = m_ref[...]
    m_new = jnp.maximum(m_old, jnp.max(s, axis=-1, keepdims=True))
    p = jnp.exp2(s - m_new)
    alpha = jnp.exp2(m_old - m_new)
    l_new = alpha * l_ref[...] + jnp.sum(p, axis=-1, keepdims=True)
    acc_new = alpha * acc_ref[...] + _dot(p.astype(BF16), ckv)
    m_ref[...] = m_new
    l_ref[...] = l_new
    acc_ref[...] = acc_new

    @pl.when(c == n_chunks - 1)
    def _():
        o_ref[0] = acc_new / l_new


def _mla_decode_call(pt_flat, qcs, ckv_new, kpe_new_t, cache_ckv, cache_kpe_t, n_pages):
    nbatch, rows, _ = qcs.shape
    page = cache_ckv.shape[1]
    chunk_pages = min(MLAD_PAGES, n_pages)
    assert n_pages % chunk_pages == 0
    chunk = chunk_pages * page
    any_spec = pl.BlockSpec(memory_space=pl.ANY)
    per_b = lambda shape: pl.BlockSpec((1,) + shape, lambda b, c, pt: (b, 0, 0))
    kernel = functools.partial(_mla_decode_kernel, n_pages=n_pages)
    return pl.pallas_call(
        kernel,
        grid_spec=pltpu.PrefetchScalarGridSpec(
            num_scalar_prefetch=1,
            grid=(nbatch, n_pages // chunk_pages),
            in_specs=[per_b((rows, QCAT)), per_b((page, KV_LORA)), per_b((MLA_ROPE_DIM, page)),
                      any_spec, any_spec],
            out_specs=per_b((rows, KV_LORA)),
            scratch_shapes=[
                pltpu.VMEM((2, chunk, KV_LORA), F32),
                pltpu.VMEM((2, MLA_ROPE_DIM, chunk), F32),
                pltpu.VMEM((rows, 1), F32),
                pltpu.VMEM((rows, 1), F32),
                pltpu.VMEM((rows, KV_LORA), F32),
                pltpu.SemaphoreType.DMA((2, 2)),
            ]),
        out_shape=jax.ShapeDtypeStruct((nbatch, rows, KV_LORA), F32),
        compiler_params=pltpu.CompilerParams(
            dimension_semantics=("arbitrary", "arbitrary"), vmem_limit_bytes=VMEM_LIMIT_BYTES),
        name="mla_decode",
    )(pt_flat, qcs, ckv_new, kpe_new_t, cache_ckv, cache_kpe_t)


def _prep_ffn(g_pre, w_gate, w_up, w_down, g_post):
    d, d_ff = w_gate.shape
    assert d_ff % FFN_CHUNK == 0
    return (g_pre.reshape(1, d), w_gate.astype(BF16), w_up.astype(BF16), w_down.astype(BF16),
            g_post.reshape(1, d))


def _prep_proj(g_mix_pre, w_in, g_q_lora, w_q_up, w_uk, g_kv_lora):
    d = w_in.shape[0]
    half = MLA_ROPE_DIM // 2
    kpe_w = w_in[:, _IN_KPE:_IN_KPE + MLA_ROPE_DIM]
    pad = jnp.zeros((d, LANES - MLA_ROPE_DIM), w_in.dtype)
    w_in_ext = jnp.concatenate([
        w_in[:, :SB_WIDTH] * SB_SCALE, w_in[:, SB_WIDTH:_IN_KPE],
        kpe_w, pad, kpe_w[:, half:], kpe_w[:, :half], pad], axis=1).astype(BF16)

    wq = w_q_up.reshape(Q_LORA, MLA_HEADS, MLA_QK_DIM)
    nope = wq[:, :, :MLA_NOPE_DIM].reshape(Q_LORA, MLA_HEADS * MLA_NOPE_DIM)
    pe = wq[:, :, MLA_NOPE_DIM:]
    pe_swap = jnp.concatenate([pe[:, :, half:], pe[:, :, :half]], axis=-1)
    lane_pad = lambda a: jnp.pad(a, ((0, 0), (0, 0), (0, LANES - MLA_ROPE_DIM))).reshape(Q_LORA, MLA_HEADS * LANES)
    w_qup_ext = jnp.concatenate([nope, lane_pad(pe), lane_pad(pe_swap)], axis=1).astype(BF16)

    eye = jnp.eye(MLA_HEADS, dtype=w_uk.dtype)
    w_uk_bd = jnp.einsum('chn,hg->hngc', w_uk, eye).reshape(MLA_HEADS * MLA_NOPE_DIM, MLA_HEADS * KV_LORA)
    return (g_mix_pre.reshape(1, d), w_in_ext, g_q_lora.reshape(1, Q_LORA), w_qup_ext,
            _pair_blocks(w_uk_bd).astype(BF16), g_kv_lora.reshape(1, KV_LORA))


def _pair_blocks(w_bd):
    n = MLA_HEADS // 2
    r, c = w_bd.shape[0] // n, w_bd.shape[1] // n
    return jnp.stack([w_bd[p * r:(p + 1) * r, p * c:(p + 1) * c] for p in range(n)])


def _prep_out(g_sb_out, g_mla_out, w_uv, w_out, g_mix_post):
    eye = jnp.eye(MLA_HEADS, dtype=w_uv.dtype)
    w_uv_bd = jnp.einsum('chv,hg->hcgv', w_uv, eye).reshape(MLA_HEADS * KV_LORA, MLA_WIDTH)
    return (g_sb_out.reshape(1, SB_WIDTH), g_mla_out.reshape(1, MLA_WIDTH), _pair_blocks(w_uv_bd).astype(BF16),
            w_out.astype(BF16), g_mix_post.reshape(1, -1))


def _rope_tables(pos):
    half = MLA_ROPE_DIM // 2
    inv_freq = ROPE_THETA ** (-jnp.arange(half, dtype=F32) / half)
    ang = pos.astype(F32)[:, None] * inv_freq[None, :]
    cos, sin = jnp.cos(ang), jnp.sin(ang)
    pad = jnp.zeros((pos.shape[0], LANES - MLA_ROPE_DIM), F32)
    return (jnp.concatenate([cos, cos, pad], axis=1), jnp.concatenate([-sin, sin, pad], axis=1))


def _token_tile(t, largest=256):
    for tm in (largest, 256, 128):
        if t % tm == 0:
            return tm
    raise ValueError(f"token count {t} is not a multiple of 128")


def kernel(x_prompt, x_sample, cache_sb_k, cache_sb_v, cache_mla_ckv, cache_mla_kpe, page_table,
           g_ffn1_pre, w_ffn1_gate, w_ffn1_up, w_ffn1_down, g_ffn1_post,
           g_mix_pre, w_in, g_q_lora, w_q_up, g_kv_lora, w_uk, w_uv, g_sb_out, g_mla_out, w_out, g_mix_post,
           g_ffn2_pre, w_ffn2_gate, w_ffn2_up, w_ffn2_down, g_ffn2_post):
    batch, seq, d = x_prompt.shape
    dec_batch, dec_seq, _ = x_sample.shape
    page = cache_sb_k.shape[1]
    n_pages = page_table.shape[1]
    past_len = n_pages * page
    assert seq % MLA_CHUNK == 0 and MLA_CHUNK % ATTN_BLOCK == 0 and dec_seq <= SUBLANES
    assert n_pages % SBD_PAGES == 0 and n_pages >= 2 * SBD_PAGES
    assert dec_batch % SBD_GROUP == 0 and page % dec_seq == 0 and (dec_batch * dec_seq) % page == 0

    ffn1 = _prep_ffn(g_ffn1_pre, w_ffn1_gate, w_ffn1_up, w_ffn1_down, g_ffn1_post)
    ffn2 = _prep_ffn(g_ffn2_pre, w_ffn2_gate, w_ffn2_up, w_ffn2_down, g_ffn2_post)
    proj_w = _prep_proj(g_mix_pre, w_in, g_q_lora, w_q_up, w_uk, g_kv_lora)
    out_w = _prep_out(g_sb_out, g_mla_out, w_uv, w_out, g_mix_post)

    cos_p, sin_p = _rope_tables(jnp.arange(seq))
    tm_p = _token_tile(seq)
    (x1_p, sbk_p, sbv_p, ckv_p, kpe_p, qa_p, ka_p, va_p, qcat_p, kcat_p, ckvt_p) = _ffn_proj_call(
        x_prompt, cos_p, sin_p, ffn1, proj_w, tm_p)
    osb_p, olat_p = _prompt_attn_call(qa_p, ka_p, va_p, qcat_p, kcat_p, ckvt_p)
    y_p = _out_ffn_call(osb_p, olat_p, x1_p, out_w, ffn2, _token_tile(seq, OUT_FFN_TILE))

    n_tok = dec_batch * dec_seq
    cos_s, sin_s = _rope_tables(past_len + (jnp.arange(n_tok) % dec_seq))
    tm_s = _token_tile(n_tok)
    (x1_s, sbk_s, sbv_s, ckv_s, kpe_s, qa_s, _, _, qcat_s, _, _) = _ffn_proj_call(
        x_sample.reshape(1, n_tok, d), cos_s, sin_s, ffn1, proj_w, tm_s)

    pt_flat = page_table.reshape(-1)
    pad_tok = page - dec_seq
    q8 = qa_s[0].transpose(1, 0, 2).reshape(dec_batch, dec_seq, SB_HEADS, SB_HEAD_DIM).transpose(0, 2, 1, 3)
    q8 = jnp.pad(q8, ((0, 0), (0, 0), (0, SUBLANES - dec_seq), (0, 0))).astype(F32)
    head_dim_tok = lambda a: a.reshape(SB_HEADS, SB_HEAD_DIM, n_tok)
    o8 = _sb_decode_call(pt_flat, q8, head_dim_tok(sbk_s), head_dim_tok(sbv_s),
                         cache_sb_k.transpose(0, 2, 3, 1), cache_sb_v.transpose(0, 2, 3, 1), n_pages, dec_seq)
    osb_s = o8[:, :, :dec_seq, :].transpose(0, 2, 1, 3).reshape(n_tok, SB_PAIRS, LANES).transpose(1, 0, 2)[None]

    qcs = qcat_s[0].reshape(MLA_HEADS, dec_batch, dec_seq, QCAT).transpose(1, 2, 0, 3)
    qcs = qcs.reshape(dec_batch, dec_seq * MLA_HEADS, QCAT)
    ckv_new = jnp.pad(ckv_s.reshape(dec_batch, dec_seq, KV_LORA), ((0, 0), (0, pad_tok), (0, 0)))
    kpe_new_t = jnp.pad(kpe_s.reshape(dec_batch, dec_seq, MLA_ROPE_DIM).transpose(0, 2, 1),
                        ((0, 0), (0, 0), (0, pad_tok)))
    olat = _mla_decode_call(pt_flat, qcs, ckv_new, kpe_new_t, cache_mla_ckv,
                            cache_mla_kpe.transpose(0, 2, 1), n_pages)
    olat_s = olat.reshape(dec_batch, dec_seq, MLA_HEADS, KV_LORA).transpose(2, 0, 1, 3)
    olat_s = olat_s.reshape(1, MLA_HEADS, n_tok, KV_LORA)

    y_s = _out_ffn_call(osb_s, olat_s, x1_s, out_w, ffn2, tm_s)

    tok_head_dim = lambda a, lead: a.reshape(lead + (SB_HEADS, SB_HEAD_DIM, -1)).transpose(
        tuple(range(len(lead))) + (len(lead) + 2, len(lead), len(lead) + 1))
    return (y_p, y_s.reshape(dec_batch, dec_seq, d),
            tok_head_dim(sbk_p, (batch,)), tok_head_dim(sbv_p, (batch,)),
            ckv_p, kpe_p,
            tok_head_dim(sbk_s[0], ()).reshape(dec_batch, dec_seq, SB_HEADS, SB_HEAD_DIM),
            tok_head_dim(sbv_s[0], ()).reshape(dec_batch, dec_seq, SB_HEADS, SB_HEAD_DIM),
            ckv_s.reshape(dec_batch, dec_seq, KV_LORA), kpe_s.reshape(dec_batch, dec_seq, MLA_ROPE_DIM))
```

```python
import functools

import jax
import jax.numpy as jnp
from jax import lax
from jax.experimental import pallas as pl
from jax.experimental.pallas import tpu as pltpu

F32 = jnp.float32
BF16 = jnp.bfloat16

SB_HEADS = 8
SB_HEAD_DIM = 64
SB_WIDTH = SB_HEADS * SB_HEAD_DIM
SB_PAIRS = SB_HEADS // 2
MLA_HEADS = 8
MLA_NOPE_DIM = 64
MLA_ROPE_DIM = 32
MLA_V_DIM = 64
MLA_QK_DIM = MLA_NOPE_DIM + MLA_ROPE_DIM
MLA_WIDTH = MLA_HEADS * MLA_V_DIM
Q_LORA = 256
KV_LORA = 128
FFN_RESIDUAL_WEIGHT = 0.5
ROPE_THETA = 10000.0
NORM_EPS = 1e-6
SB_SCALE = SB_HEAD_DIM ** -0.5
MLA_SCALE = MLA_QK_DIM ** -0.5
LOG2_E = 1.4426950408889634
MLA_EXP2_SCALE = MLA_SCALE * LOG2_E
NEG_INF = -1e30

LANES = 128
SUBLANES = 8
MXU_DIM = 256
VMEM_LIMIT_BYTES = 56 * 1024 * 1024

FFN_CHUNK = MXU_DIM
OUT_FFN_TILE = 2 * MXU_DIM
ATTN_BLOCK = MXU_DIM
MLA_CHUNK = 2 * MXU_DIM
CKVT_ROWS = KV_LORA + 16
QCAT = 2 * LANES
SB_EXIT = 104.0

SBD_GROUP = 4
SBD_PAGES = 2
MLAD_PAGES = 64


def _dot(a, b):
    return jnp.dot(a, b, preferred_element_type=F32)


def _dot_nt(a, b):
    return lax.dot_general(a, b, (((1,), (1,)), ((), ())), preferred_element_type=F32)


def _rms(x, g):
    ms = jnp.mean(x * x, axis=-1, keepdims=True)
    return x * lax.rsqrt(ms + NORM_EPS) * g


def _softplus(z):
    return jnp.maximum(z, 0.0) + jnp.log(1.0 + jnp.exp(-jnp.abs(z)))


def _split_bf16(x):
    hi = x.astype(BF16)
    lo = (x - hi.astype(F32)).astype(BF16)
    return hi, lo


def _suffix_sums(x, tri):
    hi, lo = _split_bf16(x)
    return _dot(hi, tri) + _dot(lo, tri)


def _tri(n):
    row = lax.broadcasted_iota(jnp.int32, (n, n), 0)
    col = lax.broadcasted_iota(jnp.int32, (n, n), 1)
    return (row >= col).astype(BF16)


def _const_spec(shape):
    zeros = (0,) * len(shape)
    return pl.BlockSpec(shape, lambda *_: zeros, pipeline_mode=pl.Buffered(1))


def _half_step_ffn(x, ffn_refs, act_ref):
    g_pre_ref, wg_ref, wu_ref, wd_ref, g_post_ref = ffn_refs
    h = _rms(x, g_pre_ref[...]).astype(BF16)
    for c in range(wg_ref.shape[1] // FFN_CHUNK):
        cols = slice(c * FFN_CHUNK, (c + 1) * FFN_CHUNK)
        g = _dot(h, wg_ref[:, cols])
        u = _dot(h, wu_ref[:, cols])
        act_ref[:, cols] = ((g * jax.nn.sigmoid(g)) * u).astype(BF16)
    y = _dot(act_ref[...], wd_ref[...])
    return x + FFN_RESIDUAL_WEIGHT * _rms(y, g_post_ref[...])


_IN_Q = 0
_IN_K = SB_WIDTH
_IN_V = 2 * SB_WIDTH
_IN_QD = 3 * SB_WIDTH
_IN_KVD = _IN_QD + Q_LORA
_IN_KPE = _IN_KVD + KV_LORA
_IN_KPE_SWAP = _IN_KPE + LANES
_IN_COLS_EXT = _IN_KPE_SWAP + LANES
_QUP_NOPE = 0
_QUP_PE = MLA_HEADS * MLA_NOPE_DIM
_QUP_PE_SWAP = _QUP_PE + MLA_HEADS * LANES
_QUP_COLS_EXT = _QUP_PE_SWAP + MLA_HEADS * LANES


def _ffn_proj_kernel(x_ref, cos_ref, sin_ref, g_pre_ref, wg_ref, wu_ref, wd_ref, g_post_ref,
                     g_mix_ref, w_in_ref, g_q_ref, w_qup_ref, w_uk_ref, g_kv_ref,
                     x1_ref, sbk_ref, sbv_ref, ckv_ref, kpe_ref,
                     qa_ref, ka_ref, va_ref, qcat_ref, kcat_ref, ckvt_ref, act_ref):
    x1 = _half_step_ffn(x_ref[0], (g_pre_ref, wg_ref, wu_ref, wd_ref, g_post_ref), act_ref)
    x1_ref[0] = x1

    hm = _rms(x1, g_mix_ref[...]).astype(BF16)
    proj = _dot(hm, w_in_ref[...])
    sbk = proj[:, _IN_K:_IN_K + SB_WIDTH]
    sbv = proj[:, _IN_V:_IN_V + SB_WIDTH]
    sbk_ref[0] = sbk.T
    sbv_ref[0] = sbv.T
    for p in range(SB_PAIRS):
        lo, hi = p * LANES, (p + 1) * LANES
        qa_ref[0, p] = proj[:, _IN_Q + lo:_IN_Q + hi].astype(BF16)
        ka_ref[0, p] = sbk[:, lo:hi].astype(BF16)
        va_ref[0, p] = sbv[:, lo:hi].astype(BF16)

    cos = cos_ref[...]
    sin = sin_ref[...]
    qn = _rms(proj[:, _IN_QD:_IN_QD + Q_LORA], g_q_ref[...]).astype(BF16)
    qup = _dot(qn, w_qup_ref[...])
    qnope = qup[:, _QUP_NOPE:_QUP_PE].astype(BF16)
    qlat = jnp.concatenate([_dot(qnope[:, p * LANES:(p + 1) * LANES], w_uk_ref[p])
                            for p in range(MLA_HEADS // 2)], axis=-1)
    for h in range(MLA_HEADS):
        lo, hi = h * LANES, (h + 1) * LANES
        pe = (qup[:, _QUP_PE + lo:_QUP_PE + hi] * cos
              + qup[:, _QUP_PE_SWAP + lo:_QUP_PE_SWAP + hi] * sin)
        qcat_ref[0, h, :, :LANES] = (qlat[:, lo:hi] * MLA_EXP2_SCALE).astype(BF16)
        qcat_ref[0, h, :, LANES:] = (pe * MLA_EXP2_SCALE).astype(BF16)

    ckv = _rms(proj[:, _IN_KVD:_IN_KVD + KV_LORA], g_kv_ref[...])
    kpe = (proj[:, _IN_KPE:_IN_KPE + LANES] * cos
           + proj[:, _IN_KPE_SWAP:_IN_KPE_SWAP + LANES] * sin)
    ckv_ref[0] = ckv
    kpe_ref[0] = kpe[:, :MLA_ROPE_DIM]
    kcat_ref[0, :, :LANES] = ckv.astype(BF16)
    kcat_ref[0, :, LANES:] = kpe.astype(BF16)
    ckvt_ref[0, :KV_LORA, :] = ckv.T.astype(BF16)
    ckvt_ref[0, KV_LORA:, :] = jnp.ones((CKVT_ROWS - KV_LORA, ckv.shape[0]), BF16)


def _ffn_proj_call(x, cos_t, sin_t, ffn_w, proj_w, tm):
    nb, t, d = x.shape
    weights = tuple(ffn_w) + tuple(proj_w)
    d_ff = ffn_w[1].shape[1]
    tok = lambda shape: pl.BlockSpec((1, tm) + shape, lambda b, i: (b, i) + (0,) * len(shape))
    heads = lambda n, w: pl.BlockSpec((1, n, tm, w), lambda b, i: (b, 0, i, 0))
    tab = pl.BlockSpec((tm, LANES), lambda b, i: (i, 0))
    in_specs = [tok((d,)), tab, tab] + [_const_spec(w.shape) for w in weights]
    out_shape = (
        jax.ShapeDtypeStruct((nb, t, d), F32),
        jax.ShapeDtypeStruct((nb, SB_WIDTH, t), F32),
        jax.ShapeDtypeStruct((nb, SB_WIDTH, t), F32),
        jax.ShapeDtypeStruct((nb, t, KV_LORA), F32),
        jax.ShapeDtypeStruct((nb, t, MLA_ROPE_DIM), F32),
        jax.ShapeDtypeStruct((nb, SB_PAIRS, t, LANES), BF16),
        jax.ShapeDtypeStruct((nb, SB_PAIRS, t, LANES), BF16),
        jax.ShapeDtypeStruct((nb, SB_PAIRS, t, LANES), BF16),
        jax.ShapeDtypeStruct((nb, MLA_HEADS, t, QCAT), BF16),
        jax.ShapeDtypeStruct((nb, t, QCAT), BF16),
        jax.ShapeDtypeStruct((nb, CKVT_ROWS, t), BF16),
    )
    tok_minor = lambda n: pl.BlockSpec((1, n, tm), lambda b, i: (b, 0, i))
    out_specs = (tok((d,)), tok_minor(SB_WIDTH), tok_minor(SB_WIDTH), tok((KV_LORA,)), tok((MLA_ROPE_DIM,)),
                 heads(SB_PAIRS, LANES), heads(SB_PAIRS, LANES), heads(SB_PAIRS, LANES),
                 heads(MLA_HEADS, QCAT), tok((QCAT,)), tok_minor(CKVT_ROWS))
    return pl.pallas_call(
        _ffn_proj_kernel,
        grid=(nb, t // tm),
        in_specs=in_specs,
        out_specs=out_specs,
        out_shape=out_shape,
        scratch_shapes=[pltpu.VMEM((tm, d_ff), BF16)],
        compiler_params=pltpu.CompilerParams(
            dimension_semantics=("parallel", "parallel"), vmem_limit_bytes=VMEM_LIMIT_BYTES),
        name="ffn_proj",
    )(x, cos_t, sin_t, *weights)


def _out_ffn_kernel(osb_ref, olat_ref, x1_ref, g_sb_ref, g_mla_ref, w_uv_ref, w_out_ref, g_mixpost_ref,
                    g_pre_ref, wg_ref, wu_ref, wd_ref, g_post_ref, y_ref, act_ref):
    osb = jnp.concatenate([osb_ref[0, p] for p in range(SB_PAIRS)], axis=-1)
    olat = jnp.concatenate([olat_ref[0, h] for h in range(MLA_HEADS)], axis=-1)
    olat = olat.astype(BF16)
    omla = jnp.concatenate([_dot(olat[:, p * 2 * KV_LORA:(p + 1) * 2 * KV_LORA], w_uv_ref[p])
                            for p in range(MLA_HEADS // 2)], axis=-1)
    merged = jnp.concatenate([_rms(osb, g_sb_ref[...]), _rms(omla, g_mla_ref[...])], axis=-1)
    o = _dot(merged.astype(BF16), w_out_ref[...])
    x2 = x1_ref[0] + _rms(o, g_mixpost_ref[...])
    y_ref[0] = _half_step_ffn(x2, (g_pre_ref, wg_ref, wu_ref, wd_ref, g_post_ref), act_ref)


def _out_ffn_call(osb, olat, x1, out_w, ffn_w, tm):
    nb, t, d = x1.shape
    d_ff = ffn_w[1].shape[1]
    heads = lambda n: pl.BlockSpec((1, n, tm, LANES), lambda b, i: (b, 0, i, 0))
    tok = pl.BlockSpec((1, tm, d), lambda b, i: (b, i, 0))
    weights = tuple(out_w) + tuple(ffn_w)
    return pl.pallas_call(
        _out_ffn_kernel,
        grid=(nb, t // tm),
        in_specs=[heads(SB_PAIRS), heads(MLA_HEADS), tok] + [_const_spec(w.shape) for w in weights],
        out_specs=tok,
        out_shape=jax.ShapeDtypeStruct((nb, t, d), F32),
        scratch_shapes=[pltpu.VMEM((tm, d_ff), BF16)],
        compiler_params=pltpu.CompilerParams(
            dimension_semantics=("parallel", "parallel"), vmem_limit_bytes=VMEM_LIMIT_BYTES),
        name="out_ffn",
    )(osb, olat, x1, *weights)


def _prompt_attn_kernel(qa_ref, ka_ref, va_ref, qcat_ref, kcat_ref, ckvt_ref, osb_ref, olat_ref,
                        sa_ref, sb_ref, m_ref, acc_ref):
    qi = pl.program_id(1)
    blk = ATTN_BLOCK
    tri = _tri(blk)
    rows_sb = lax.broadcasted_iota(jnp.int32, (SB_HEADS * blk, blk), 0)
    strict = lax.broadcasted_iota(jnp.int32, (SB_HEADS * blk, blk), 1) < rows_sb % blk
    even_q = jnp.logical_xor(lax.broadcasted_iota(jnp.int32, (2 * blk, LANES), 1) >= SB_HEAD_DIM,
                             lax.broadcasted_iota(jnp.int32, (2 * blk, LANES), 0) < blk)
    diag = pl.multiple_of(qi * blk, blk)

    def pair_operands(p, off):
        k = ka_ref[0, p, pl.ds(off, blk), :]
        v = va_ref[0, p, pl.ds(off, blk), :]
        v2 = jnp.concatenate([v, v], axis=0)
        return k, jnp.where(even_q, v2, jnp.zeros_like(v2))

    q2s = []
    for p in range(SB_PAIRS):
        qp = qa_ref[0, p]
        qp2 = jnp.concatenate([qp, qp], axis=0)
        q2s.append(jnp.where(even_q, qp2, jnp.zeros_like(qp2)))

    def sb_step(off, carry, accs, masked):
        operands = [pair_operands(p, off) for p in range(SB_PAIRS)]
        z = jnp.concatenate([_dot_nt(q2s[p], operands[p][0]) for p in range(SB_PAIRS)], axis=0)
        sp = _softplus(z)
        if masked:
            sp = jnp.where(strict, sp, 0.0)
        suf = _dot(sp.astype(BF16), tri)
        a = jnp.exp(z - suf - carry)
        if masked:
            a = jnp.where(strict, a, 0.0)
        a = a.astype(BF16)
        out = []
        for p in range(SB_PAIRS):
            a_even = a[2 * p * blk:(2 * p + 1) * blk]
            a_odd = a[(2 * p + 1) * blk:(2 * p + 2) * blk]
            out.append(accs[p] + _dot(jnp.concatenate([a_even, a_odd], axis=1), operands[p][1]))
        return carry + suf[:, 0:1], tuple(out)

    a0 = tuple(jnp.zeros((blk, LANES), F32) for _ in range(SB_PAIRS))
    carry, accs = sb_step(diag, jnp.zeros((SB_HEADS * blk, 1), F32), a0, True)

    def sb_cond(st):
        j, carry, _ = st
        return jnp.logical_and(j >= 0, jnp.min(carry) < SB_EXIT)

    def sb_body(st):
        j, carry, accs = st
        carry, accs = sb_step(pl.multiple_of(j * blk, blk), carry, accs, False)
        return j - 1, carry, accs

    _, _, accs = lax.while_loop(sb_cond, sb_body, (qi - 1, carry, accs))
    for p in range(SB_PAIRS):
        osb_ref[0, p] = accs[p]

    ch = MLA_CHUNK

    cols = MLA_HEADS * blk
    qc = qcat_ref[0].reshape(cols, QCAT)

    def scores_into(dst_ref, j):
        off = pl.multiple_of(j * ch, ch)
        dst_ref[...] = _dot_nt(kcat_ref[0, pl.ds(off, ch), :], qc)

    def consume(src_ref, j, masked):
        off = pl.multiple_of(j * ch, ch)
        s = src_ref[...]
        if masked:
            k_pos = off + lax.broadcasted_iota(jnp.int32, (ch, cols), 0)
            s = jnp.where(k_pos <= diag + lax.broadcasted_iota(jnp.int32, (ch, cols), 1) % blk, s, NEG_INF)
        m = m_ref[...]
        m_new = jnp.maximum(m, jnp.max(s, axis=0, keepdims=True))
        p = jnp.exp2(s - m_new)
        acc_ref[...] = jnp.exp2(m - m_new) * acc_ref[...] + _dot(ckvt_ref[0, :, pl.ds(off, ch)], p.astype(BF16))
        m_ref[...] = m_new

    n_full = diag // ch
    m_ref[...] = jnp.full((1, cols), NEG_INF, F32)
    acc_ref[...] = jnp.zeros((CKVT_ROWS, cols), F32)
    scores_into(sa_ref, 0)

    def pair_body(i, _):
        scores_into(sb_ref, 2 * i + 1)
        consume(sa_ref, 2 * i, False)
        scores_into(sa_ref, 2 * i + 2)
        consume(sb_ref, 2 * i + 1, False)
        return 0

    lax.fori_loop(0, n_full // 2, pair_body, 0)
    odd = n_full % 2 == 1

    @pl.when(odd)
    def _():
        scores_into(sb_ref, n_full)
        consume(sa_ref, n_full - 1, False)
        consume(sb_ref, n_full, True)

    @pl.when(jnp.logical_not(odd))
    def _():
        consume(sa_ref, n_full, True)

    acc = acc_ref[...]
    o_t = acc[:KV_LORA] / acc[KV_LORA:KV_LORA + 1]
    for h in range(MLA_HEADS):
        olat_ref[0, h] = o_t[:, h * blk:(h + 1) * blk].T


def _prompt_attn_call(qa, ka, va, qcat, kcat, ckvt):
    nb, _, t, _ = qa.shape
    blk = ATTN_BLOCK
    q_spec = lambda n, w: pl.BlockSpec((1, n, blk, w), lambda b, i: (b, 0, i, 0))
    kv_spec = pl.BlockSpec((1, SB_PAIRS, t, LANES), lambda b, i: (b, 0, 0, 0))
    return pl.pallas_call(
        _prompt_attn_kernel,
        grid=(nb, t // blk),
        in_specs=[q_spec(SB_PAIRS, LANES), kv_spec, kv_spec, q_spec(MLA_HEADS, QCAT),
                  pl.BlockSpec((1, t, QCAT), lambda b, i: (b, 0, 0)),
                  pl.BlockSpec((1, CKVT_ROWS, t), lambda b, i: (b, 0, 0))],
        out_specs=(q_spec(SB_PAIRS, LANES), q_spec(MLA_HEADS, KV_LORA)),
        out_shape=(jax.ShapeDtypeStruct((nb, SB_PAIRS, t, LANES), F32),
                   jax.ShapeDtypeStruct((nb, MLA_HEADS, t, KV_LORA), F32)),
        scratch_shapes=[pltpu.VMEM((MLA_CHUNK, MLA_HEADS * blk), F32),
                        pltpu.VMEM((MLA_CHUNK, MLA_HEADS * blk), F32),
                        pltpu.VMEM((1, MLA_HEADS * blk), F32),
                        pltpu.VMEM((CKVT_ROWS, MLA_HEADS * blk), F32)],
        compiler_params=pltpu.CompilerParams(
            dimension_semantics=("parallel", "parallel"), vmem_limit_bytes=VMEM_LIMIT_BYTES),
        name="prompt_attn",
    )(qa, ka, va, qcat, kcat, ckvt)


def _sb_decode_kernel(pt_ref, q_ref, knew_hbm, vnew_hbm, kc_hbm, vc_hbm, o_ref,
                      kbuf, vbuf, kslow, vslow, sem, slow_sem, *, n_pages, n_new):
    step = pl.program_id(0)
    n_steps = pl.num_programs(0)
    grp = q_ref.shape[0]
    page = kc_hbm.shape[-1]
    first = SBD_PAGES * page
    ntok = first + page

    def first_copies(s, slot):
        out = []
        for g in range(grp):
            b = s * grp + g
            for hbm, new, buf, t in ((kc_hbm, knew_hbm, kbuf, 0), (vc_hbm, vnew_hbm, vbuf, 1)):
                for j in range(SBD_PAGES):
                    pid = pt_ref[b * n_pages + (n_pages - SBD_PAGES + j)]
                    out.append(pltpu.make_async_copy(
                        hbm.at[pid], buf.at[slot * grp + g, :, :, pl.ds(j * page, page)], sem.at[slot, t]))
                tile = pl.multiple_of((b * n_new) // page * page, page)
                out.append(pltpu.make_async_copy(
                    new.at[:, :, pl.ds(tile, page)], buf.at[slot * grp + g, :, :, pl.ds(first, page)],
                    sem.at[slot, t]))
        return out

    @pl.when(step == 0)
    def _():
        for c in first_copies(0, 0):
            c.start()

    slot = step % 2

    @pl.when(step + 1 < n_steps)
    def _():
        for c in first_copies(step + 1, 1 - slot):
            c.start()

    for c in first_copies(step, slot):
        c.wait()

    rows = SB_HEADS * SUBLANES
    r_iota = lax.broadcasted_iota(jnp.int32, (rows, ntok), 0)
    c_iota = lax.broadcasted_iota(jnp.int32, (rows, ntok), 1)
    real_row = (lax.broadcasted_iota(jnp.int32, (rows, 1), 0) % SUBLANES) < n_new
    tri = _tri(ntok)
    tri_slow = _tri(first)

    def scores(qs, buf, idx):
        return jnp.concatenate([_dot(qs[h], buf[idx, h].astype(BF16)) for h in range(SB_HEADS)], axis=0)

    def weighted(a, buf, idx):
        v_all = buf[idx].reshape(SB_WIDTH, -1).astype(BF16)
        full = _dot_nt(a, v_all)
        return jnp.concatenate(
            [full[h * SUBLANES:(h + 1) * SUBLANES, h * SB_HEAD_DIM:(h + 1) * SB_HEAD_DIM]
             for h in range(SB_HEADS)], axis=0)

    qss, zs, valids = [], [], []
    for g in range(grp):
        j_new = c_iota - (first + ((step * grp + g) * n_new) % page)
        valids.append(jnp.logical_or(c_iota < first, jnp.logical_and(j_new >= 0, j_new < r_iota % SUBLANES)))
        qss.append([q_ref[g, h].astype(BF16) for h in range(SB_HEADS)])
        zs.append(scores(qss[g], kbuf, slot * grp + g))
    sp = jnp.concatenate([jnp.where(valids[g], _softplus(zs[g]), 0.0) for g in range(grp)], axis=0)
    suf_all = _suffix_sums(sp, tri)
    newest = []
    for g in range(grp):
        suf = suf_all[g * rows:(g + 1) * rows]
        a = jnp.where(valids[g], jnp.exp(zs[g] - suf), 0.0).astype(BF16)
        newest.append((qss[g], suf[:, 0:1], weighted(a, vbuf, slot * grp + g)))

    for g in range(grp):
        b = step * grp + g
        qs, carry, acc = newest[g]

        def cond(st):
            c, carry, _ = st
            live = jnp.min(jnp.where(real_row, carry, SB_EXIT)) < SB_EXIT
            return jnp.logical_and(c >= 0, live)

        def body(st):
            c, carry, acc = st
            copies = []
            for hbm, buf, t in ((kc_hbm, kslow, 0), (vc_hbm, vslow, 1)):
                for j in range(SBD_PAGES):
                    pid = pt_ref[b * n_pages + c * SBD_PAGES + j]
                    copies.append(pltpu.make_async_copy(
                        hbm.at[pid], buf.at[0, :, :, pl.ds(j * page, page)], slow_sem.at[t]))
            for cp in copies:
                cp.start()
            for cp in copies:
                cp.wait()
            z = scores(qs, kslow, 0)
            suf = _suffix_sums(_softplus(z), tri_slow)
            a = jnp.exp(z - suf - carry).astype(BF16)
            return c - 1, carry + suf[:, 0:1], acc + weighted(a, vslow, 0)

        _, _, acc = lax.while_loop(cond, body, (n_pages // SBD_PAGES - 2, carry, acc))
        o_ref[g] = acc.reshape(SB_HEADS, SUBLANES, SB_HEAD_DIM)


def _sb_decode_call(pt_flat, q8, knew_t, vnew_t, cache_kt, cache_vt, n_pages, n_new):
    nbatch = q8.shape[0]
    page = cache_kt.shape[-1]
    grp = SBD_GROUP
    ntok = (SBD_PAGES + 1) * page
    any_spec = pl.BlockSpec(memory_space=pl.ANY)
    blk = pl.BlockSpec((grp, SB_HEADS, SUBLANES, SB_HEAD_DIM), lambda s, pt: (s, 0, 0, 0))
    kernel = functools.partial(_sb_decode_kernel, n_pages=n_pages, n_new=n_new)
    return pl.pallas_call(
        kernel,
        grid_spec=pltpu.PrefetchScalarGridSpec(
            num_scalar_prefetch=1,
            grid=(nbatch // grp,),
            in_specs=[blk, any_spec, any_spec, any_spec, any_spec],
            out_specs=blk,
            scratch_shapes=[
                pltpu.VMEM((2 * grp, SB_HEADS, SB_HEAD_DIM, ntok), F32),
                pltpu.VMEM((2 * grp, SB_HEADS, SB_HEAD_DIM, ntok), F32),
                pltpu.VMEM((1, SB_HEADS, SB_HEAD_DIM, SBD_PAGES * page), F32),
                pltpu.VMEM((1, SB_HEADS, SB_HEAD_DIM, SBD_PAGES * page), F32),
                pltpu.SemaphoreType.DMA((2, 2)),
                pltpu.SemaphoreType.DMA((2,)),
            ]),
        out_shape=jax.ShapeDtypeStruct((nbatch, SB_HEADS, SUBLANES, SB_HEAD_DIM), F32),
        compiler_params=pltpu.CompilerParams(
            dimension_semantics=("arbitrary",), vmem_limit_bytes=VMEM_LIMIT_BYTES),
        name="sb_decode",
    )(pt_flat, q8, knew_t, vnew_t, cache_kt, cache_vt)


def _mla_decode_kernel(pt_ref, q_ref, ckvnew_ref, kpenew_ref, ckv_hbm, kpe_hbm, o_ref,
                       ckvbuf, kpebuf, m_ref, l_ref, acc_ref, sem, *, n_pages):
    b = pl.program_id(0)
    c = pl.program_id(1)
    n_chunks = pl.num_programs(1)
    n = b * n_chunks + c
    total = pl.num_programs(0) * n_chunks
    page = ckv_hbm.shape[1]
    chunk_pages = ckvbuf.shape[1] // page

    def copies(i, slot):
        bb = i // n_chunks
        cc = i % n_chunks
        out = []
        for j in range(chunk_pages):
            pid = pt_ref[bb * n_pages + cc * chunk_pages + j]
            out.append(pltpu.make_async_copy(
                ckv_hbm.at[pid], ckvbuf.at[slot, pl.ds(j * page, page), :], sem.at[slot, 0]))
            out.append(pltpu.make_async_copy(
                kpe_hbm.at[pid], kpebuf.at[slot, :, pl.ds(j * page, page)], sem.at[slot, 1]))
        return out

    @pl.when(n == 0)
    def _():
        for cp in copies(0, 0):
            cp.start()

    slot = n % 2

    @pl.when(n + 1 < total)
    def _():
        for cp in copies(n + 1, 1 - slot):
            cp.start()

    q = q_ref[0]
    q_lat = q[:, :KV_LORA]
    q_pe = q[:, KV_LORA:KV_LORA + MLA_ROPE_DIM]

    def scores(ckv, kpe_t):
        return _dot_nt(q_lat, ckv) + _dot(q_pe, kpe_t)

    @pl.when(c == 0)
    def _():
        ckv = ckvnew_ref[0].astype(BF16)
        s = scores(ckv, kpenew_ref[0].astype(BF16))
        r_iota = lax.broadcasted_iota(jnp.int32, s.shape, 0)
        c_iota = lax.broadcasted_iota(jnp.int32, s.shape, 1)
        s = jnp.where(c_iota <= r_iota // MLA_HEADS, s, NEG_INF)
        m = jnp.max(s, axis=-1, keepdims=True)
        p = jnp.exp2(s - m)
        m_ref[...] = m
        l_ref[...] = jnp.sum(p, axis=-1, keepdims=True)
        acc_ref[...] = _dot(p.astype(BF16), ckv)

    for cp in copies(n, slot):
        cp.wait()

    ckv = ckvbuf[slot].astype(BF16)
    s = scores(ckv, kpebuf[slot].astype(BF16))
    m_old = m_ref[...]
    m_new = jnp.maximum(m_old, jnp.max(s, axis=-1, keepdims=True))
    p = jnp.exp2(s - m_new)
    alpha = jnp.exp2(m_old - m_new)
    l_new = alpha * l_ref[...] + jnp.sum(p, axis=-1, keepdims=True)
    acc_new = alpha * acc_ref[...] + _dot(p.astype(BF16), ckv)
    m_ref[...] = m_new
    l_ref[...] = l_new
    acc_ref[...] = acc_new

    @pl.when(c == n_chunks - 1)
    def _():
        o_ref[0] = acc_new / l_new


def _mla_decode_call(pt_flat, qcs, ckv_new, kpe_new_t, cache_ckv, cache_kpe_t, n_pages):
    nbatch, rows, _ = qcs.shape
    page = cache_ckv.shape[1]
    chunk_pages = min(MLAD_PAGES, n_pages)
    assert n_pages % chunk_pages == 0
    chunk = chunk_pages * page
    any_spec = pl.BlockSpec(memory_space=pl.ANY)
    per_b = lambda shape: pl.BlockSpec((1,) + shape, lambda b, c, pt: (b, 0, 0))
    kernel = functools.partial(_mla_decode_kernel, n_pages=n_pages)
    return pl.pallas_call(
        kernel,
        grid_spec=pltpu.PrefetchScalarGridSpec(
            num_scalar_prefetch=1,
            grid=(nbatch, n_pages // chunk_pages),
            in_specs=[per_b((rows, QCAT)), per_b((page, KV_LORA)), per_b((MLA_ROPE_DIM, page)),
                      any_spec, any_spec],
            out_specs=per_b((rows, KV_LORA)),
            scratch_shapes=[
                pltpu.VMEM((2, chunk, KV_LORA), F32),
                pltpu.VMEM((2, MLA_ROPE_DIM, chunk), F32),
                pltpu.VMEM((rows, 1), F32),
                pltpu.VMEM((rows, 1), F32),
                pltpu.VMEM((rows, KV_LORA), F32),
                pltpu.SemaphoreType.DMA((2, 2)),
            ]),
        out_shape=jax.ShapeDtypeStruct((nbatch, rows, KV_LORA), F32),
        compiler_params=pltpu.CompilerParams(
            dimension_semantics=("arbitrary", "arbitrary"), vmem_limit_bytes=VMEM_LIMIT_BYTES),
        name="mla_decode",
    )(pt_flat, qcs, ckv_new, kpe_new_t, cache_ckv, cache_kpe_t)


def _prep_ffn(g_pre, w_gate, w_up, w_down, g_post):
    d, d_ff = w_gate.shape
    assert d_ff % FFN_CHUNK == 0
    return (g_pre.reshape(1, d), w_gate.astype(BF16), w_up.astype(BF16), w_down.astype(BF16),
            g_post.reshape(1, d))


def _prep_proj(g_mix_pre, w_in, g_q_lora, w_q_up, w_uk, g_kv_lora):
    d = w_in.shape[0]
    half = MLA_ROPE_DIM // 2
    kpe_w = w_in[:, _IN_KPE:_IN_KPE + MLA_ROPE_DIM]
    pad = jnp.zeros((d, LANES - MLA_ROPE_DIM), w_in.dtype)
    w_in_ext = jnp.concatenate([
        w_in[:, :SB_WIDTH] * SB_SCALE, w_in[:, SB_WIDTH:_IN_KPE],
        kpe_w, pad, kpe_w[:, half:], kpe_w[:, :half], pad], axis=1).astype(BF16)

    wq = w_q_up.reshape(Q_LORA, MLA_HEADS, MLA_QK_DIM)
    nope = wq[:, :, :MLA_NOPE_DIM].reshape(Q_LORA, MLA_HEADS * MLA_NOPE_DIM)
    pe = wq[:, :, MLA_NOPE_DIM:]
    pe_swap = jnp.concatenate([pe[:, :, half:], pe[:, :, :half]], axis=-1)
    lane_pad = lambda a: jnp.pad(a, ((0, 0), (0, 0), (0, LANES - MLA_ROPE_DIM))).reshape(Q_LORA, MLA_HEADS * LANES)
    w_qup_ext = jnp.concatenate([nope, lane_pad(pe), lane_pad(pe_swap)], axis=1).astype(BF16)

    eye = jnp.eye(MLA_HEADS, dtype=w_uk.dtype)
    w_uk_bd = jnp.einsum('chn,hg->hngc', w_uk, eye).reshape(MLA_HEADS * MLA_NOPE_DIM, MLA_HEADS * KV_LORA)
    return (g_mix_pre.reshape(1, d), w_in_ext, g_q_lora.reshape(1, Q_LORA), w_qup_ext,
            _pair_blocks(w_uk_bd).astype(BF16), g_kv_lora.reshape(1, KV_LORA))


def _pair_blocks(w_bd):
    n = MLA_HEADS // 2
    r, c = w_bd.shape[0] // n, w_bd.shape[1] // n
    return jnp.stack([w_bd[p * r:(p + 1) * r, p * c:(p + 1) * c] for p in range(n)])


def _prep_out(g_sb_out, g_mla_out, w_uv, w_out, g_mix_post):
    eye = jnp.eye(MLA_HEADS, dtype=w_uv.dtype)
    w_uv_bd = jnp.einsum('chv,hg->hcgv', w_uv, eye).reshape(MLA_HEADS * KV_LORA, MLA_WIDTH)
    return (g_sb_out.reshape(1, SB_WIDTH), g_mla_out.reshape(1, MLA_WIDTH), _pair_blocks(w_uv_bd).astype(BF16),
            w_out.astype(BF16), g_mix_post.reshape(1, -1))


def _rope_tables(pos):
    half = MLA_ROPE_DIM // 2
    inv_freq = ROPE_THETA ** (-jnp.arange(half, dtype=F32) / half)
    ang = pos.astype(F32)[:, None] * inv_freq[None, :]
    cos, sin = jnp.cos(ang), jnp.sin(ang)
    pad = jnp.zeros((pos.shape[0], LANES - MLA_ROPE_DIM), F32)
    return (jnp.concatenate([cos, cos, pad], axis=1), jnp.concatenate([-sin, sin, pad], axis=1))


def _token_tile(t, largest=256):
    for tm in (largest, 256, 128):
        if t % tm == 0:
            return tm
    raise ValueError(f"token count {t} is not a multiple of 128")


def kernel(x_prompt, x_sample, cache_sb_k, cache_sb_v, cache_mla_ckv, cache_mla_kpe, page_table,
           g_ffn1_pre, w_ffn1_gate, w_ffn1_up, w_ffn1_down, g_ffn1_post,
           g_mix_pre, w_in, g_q_lora, w_q_up, g_kv_lora, w_uk, w_uv, g_sb_out, g_mla_out, w_out, g_mix_post,
           g_ffn2_pre, w_ffn2_gate, w_ffn2_up, w_ffn2_down, g_ffn2_post):
    batch, seq, d = x_prompt.shape
    dec_batch, dec_seq, _ = x_sample.shape
    page = cache_sb_k.shape[1]
    n_pages = page_table.shape[1]
    past_len = n_pages * page
    assert seq % MLA_CHUNK == 0 and MLA_CHUNK % ATTN_BLOCK == 0 and dec_seq <= SUBLANES
    assert n_pages % SBD_PAGES == 0 and n_pages >= 2 * SBD_PAGES
    assert dec_batch % SBD_GROUP == 0 and page % dec_seq == 0 and (dec_batch * dec_seq) % page == 0

    ffn1 = _prep_ffn(g_ffn1_pre, w_ffn1_gate, w_ffn1_up, w_ffn1_down, g_ffn1_post)
    ffn2 = _prep_ffn(g_ffn2_pre, w_ffn2_gate, w_ffn2_up, w_ffn2_down, g_ffn2_post)
    proj_w = _prep_proj(g_mix_pre, w_in, g_q_lora, w_q_up, w_uk, g_kv_lora)
    out_w = _prep_out(g_sb_out, g_mla_out, w_uv, w_out, g_mix_post)

    cos_p, sin_p = _rope_tables(jnp.arange(seq))
    tm_p = _token_tile(seq, OUT_FFN_TILE)
    (x1_p, sbk_p, sbv_p, ckv_p, kpe_p, qa_p, ka_p, va_p, qcat_p, kcat_p, ckvt_p) = _ffn_proj_call(
        x_prompt, cos_p, sin_p, ffn1, proj_w, tm_p)
    osb_p, olat_p = _prompt_attn_call(qa_p, ka_p, va_p, qcat_p, kcat_p, ckvt_p)
    y_p = _out_ffn_call(osb_p, olat_p, x1_p, out_w, ffn2, _token_tile(seq, OUT_FFN_TILE))

    n_tok = dec_batch * dec_seq
    cos_s, sin_s = _rope_tables(past_len + (jnp.arange(n_tok) % dec_seq))
    tm_s = _token_tile(n_tok)
    (x1_s, sbk_s, sbv_s, ckv_s, kpe_s, qa_s, _, _, qcat_s, _, _) = _ffn_proj_call(
        x_sample.reshape(1, n_tok, d), cos_s, sin_s, ffn1, proj_w, tm_s)

    pt_flat = page_table.reshape(-1)
    pad_tok = page - dec_seq
    q8 = qa_s[0].transpose(1, 0, 2).reshape(dec_batch, dec_seq, SB_HEADS, SB_HEAD_DIM).transpose(0, 2, 1, 3)
    q8 = jnp.pad(q8, ((0, 0), (0, 0), (0, SUBLANES - dec_seq), (0, 0))).astype(F32)
    head_dim_tok = lambda a: a.reshape(SB_HEADS, SB_HEAD_DIM, n_tok)
    o8 = _sb_decode_call(pt_flat, q8, head_dim_tok(sbk_s), head_dim_tok(sbv_s),
                         cache_sb_k.transpose(0, 2, 3, 1), cache_sb_v.transpose(0, 2, 3, 1), n_pages, dec_seq)
    osb_s = o8[:, :, :dec_seq, :].transpose(0, 2, 1, 3).reshape(n_tok, SB_PAIRS, LANES).transpose(1, 0, 2)[None]

    qcs = qcat_s[0].reshape(MLA_HEADS, dec_batch, dec_seq, QCAT).transpose(1, 2, 0, 3)
    qcs = qcs.reshape(dec_batch, dec_seq * MLA_HEADS, QCAT)
    ckv_new = jnp.pad(ckv_s.reshape(dec_batch, dec_seq, KV_LORA), ((0, 0), (0, pad_tok), (0, 0)))
    kpe_new_t = jnp.pad(kpe_s.reshape(dec_batch, dec_seq, MLA_ROPE_DIM).transpose(0, 2, 1),
                        ((0, 0), (0, 0), (0, pad_tok)))
    olat = _mla_decode_call(pt_flat, qcs, ckv_new, kpe_new_t, cache_mla_ckv,
                            cache_mla_kpe.transpose(0, 2, 1), n_pages)
    olat_s = olat.reshape(dec_batch, dec_seq, MLA_HEADS, KV_LORA).transpose(2, 0, 1, 3)
    olat_s = olat_s.reshape(1, MLA_HEADS, n_tok, KV_LORA)

    y_s = _out_ffn_call(osb_s, olat_s, x1_s, out_w, ffn2, tm_s)

    tok_head_dim = lambda a, lead: a.reshape(lead + (SB_HEADS, SB_HEAD_DIM, -1)).transpose(
        tuple(range(len(lead))) + (len(lead) + 2, len(lead), len(lead) + 1))
    return (y_p, y_s.reshape(dec_batch, dec_seq, d),
            tok_head_dim(sbk_p, (batch,)), tok_head_dim(sbv_p, (batch,)),
            ckv_p, kpe_p,
            tok_head_dim(sbk_s[0], ()).reshape(dec_batch, dec_seq, SB_HEADS, SB_HEAD_DIM),
            tok_head_dim(sbv_s[0], ()).reshape(dec_batch, dec_seq, SB_HEADS, SB_HEAD_DIM),
            ckv_s.reshape(dec_batch, dec_seq, KV_LORA), kpe_s.reshape(dec_batch, dec_seq, MLA_ROPE_DIM))
```
